```python
import jax, jax.numpy as jnp
from jax import lax
import numpy as np

D_MODEL = 1024
BATCH = 2
SEQ = 8192
DEPTH = 4

GRID_W = 64
CTX_LEN = 256
HEAD_DIM = 64
HALF_ROT = HEAD_DIM // 2
AXIS_FREQS = HEAD_DIM // 4
ROPE_THETA = 10000.0
RW_HEADS = 6
RW_DIM = RW_HEADS * HEAD_DIM
RW_LORA_W = 64
RW_LORA_A = 64
AT_HEADS = 6
AT_KV_HEADS = 2
AT_GROUP = AT_HEADS // AT_KV_HEADS
AT_DIM = AT_HEADS * HEAD_DIM
AT_KV_DIM = AT_KV_HEADS * HEAD_DIM
WINDOW = 128
BLOCK = 128
CV_GROUPS = 4
CV_DIM = CV_GROUPS * HEAD_DIM
CONV_W = 3
D_MIX = RW_DIM + AT_DIM + CV_DIM
IN_SPLITS = (RW_DIM, RW_DIM, RW_DIM, RW_DIM, 2 * RW_LORA_W, 2 * RW_LORA_A,
             AT_DIM, AT_KV_DIM, AT_KV_DIM, AT_DIM,
             CV_DIM, CV_DIM, CV_DIM, CV_DIM)
D_IN = 4 * RW_DIM + 2 * RW_LORA_W + 2 * RW_LORA_A + 2 * AT_DIM + 2 * AT_KV_DIM + 4 * CV_DIM
NORM_EPS = 1e-6
LNX_EPS = 64e-5
MASK_VALUE = -1e30

kernel_name = "hybrid_rwkv7_swa_shortconv_dit"


def rmsnorm(x, g):
    xf = x.astype(jnp.float32)
    y = xf * lax.rsqrt(jnp.mean(xf * xf, axis=-1, keepdims=True) + NORM_EPS)
    return (y * g.astype(jnp.float32)).astype(x.dtype)


def to_heads(t):
    return t.reshape(t.shape[:-1] + (-1, HEAD_DIM))


def axial_rope_tables(row, col):
    inv = ROPE_THETA ** (-jnp.arange(AXIS_FREQS, dtype=jnp.float32) / AXIS_FREQS)
    ang = jnp.concatenate([row.astype(jnp.float32)[:, None] * inv,
                           col.astype(jnp.float32)[:, None] * inv], axis=-1)
    return jnp.cos(ang), jnp.sin(ang)


def apply_rope(t, cos, sin):
    tf = t.astype(jnp.float32)
    t1, t2 = tf[..., :HALF_ROT], tf[..., HALF_ROT:]
    c, s = cos[:, None, :], sin[:, None, :]
    return jnp.concatenate([t1 * c - t2 * s, t1 * s + t2 * c], axis=-1).astype(t.dtype)


def rwkv7_scan(r, w, k, v, kk, a, s0, reverse):
    def step(S, inp):
        r_t, w_t, k_t, v_t, kk_t, a_t = inp
        sa = jnp.einsum("bhvk,bhk->bhv", S, kk_t)
        S = (S * w_t[:, :, None, :] - sa[..., None] * (kk_t * a_t)[:, :, None, :]
             + v_t[..., None] * k_t[:, :, None, :])
        return S, jnp.einsum("bhvk,bhk->bhv", S, r_t)
    xs = tuple(jnp.moveaxis(t.astype(jnp.float32), 1, 0) for t in (r, w, k, v, kk, a))
    s_final, ys = lax.scan(step, s0, xs, reverse=reverse)
    return jnp.moveaxis(ys, 0, 1), s_final


def rwkv7_branch(parts, w0, w_up, a0, a_up, k_k, k_a, r_k, ln_g, ln_b, s0):
    r, k, v, g, wd, ad = parts
    r, k, v = to_heads(r), to_heads(k), to_heads(v)
    kk = (k * k_k.reshape(RW_HEADS, HEAD_DIM)).astype(jnp.float32)
    kk = kk * lax.rsqrt(jnp.maximum(jnp.sum(kk * kk, axis=-1, keepdims=True), 1e-12))
    ys, states = [], []
    for d in range(2):
        lw = (w0[d] + jnp.tanh(wd[..., d * RW_LORA_W:(d + 1) * RW_LORA_W]) @ w_up[d]).astype(jnp.float32)
        decay = jnp.exp(-jnp.exp(-jax.nn.softplus(-lw) - 0.5))
        a = to_heads(jax.nn.sigmoid(a0[d] + ad[..., d * RW_LORA_A:(d + 1) * RW_LORA_A] @ a_up[d]))
        k_d = k * (1 + (a - 1) * k_a.reshape(RW_HEADS, HEAD_DIM))
        y_d, s_d = rwkv7_scan(r, to_heads(decay), k_d, v, kk, a, s0[d], reverse=(d == 1))
        ys.append(y_d)
        states.append(s_d)
    y = ys[0] + ys[1]
    mu = jnp.mean(y, axis=-1, keepdims=True)
    var = jnp.mean(jnp.square(y - mu), axis=-1, keepdims=True)
    y = ((y - mu) * lax.rsqrt(var + LNX_EPS) * ln_g.reshape(RW_HEADS, HEAD_DIM).astype(jnp.float32)
         + ln_b.reshape(RW_HEADS, HEAD_DIM).astype(jnp.float32))
    bonus = jnp.sum((r * k * r_k).astype(jnp.float32), axis=-1, keepdims=True) * v.astype(jnp.float32)
    y = (y + bonus).reshape(y.shape[:2] + (RW_DIM,)).astype(g.dtype) * jax.nn.silu(g)
    return y, (states[0], states[1])


def sink_softmax(logits, sink):
    s = jnp.broadcast_to(sink.astype(jnp.float32).reshape(AT_KV_HEADS, AT_GROUP, 1, 1),
                         logits.shape[:-1] + (1,))
    return jax.nn.softmax(jnp.concatenate([logits, s], axis=-1), axis=-1)[..., :-1]


def context_attention(qc, kc, vc, sink):
    b, n = qc.shape[:2]
    qg = qc.reshape(b, n, AT_KV_HEADS, AT_GROUP, HEAD_DIM)
    s = jnp.einsum("bqhgd,bkhd->bhgqk", qg, kc).astype(jnp.float32) * HEAD_DIM ** -0.5
    p = sink_softmax(s, sink).astype(vc.dtype)
    return jnp.einsum("bhgqk,bkhd->bqhgd", p, vc).reshape(b, n, AT_DIM)


def window_attention(q, k, v, kc, vc, sink):
    b, L = q.shape[:2]
    nb = L // BLOCK
    qb = q.reshape(b, nb, BLOCK, AT_KV_HEADS, AT_GROUP, HEAD_DIM)
    pad = ((0, 0), (BLOCK, BLOCK), (0, 0), (0, 0))
    kp = jnp.pad(k, pad).reshape(b, nb + 2, BLOCK, AT_KV_HEADS, HEAD_DIM)
    vp = jnp.pad(v, pad).reshape(b, nb + 2, BLOCK, AT_KV_HEADS, HEAD_DIM)
    kw = jnp.concatenate([kp[:, :-2], kp[:, 1:-1], kp[:, 2:]], axis=2)
    vw = jnp.concatenate([vp[:, :-2], vp[:, 1:-1], vp[:, 2:]], axis=2)
    scale = HEAD_DIM ** -0.5
    s_loc = jnp.einsum("bnqhgd,bnkhd->bnhgqk", qb, kw).astype(jnp.float32) * scale
    qpos = jnp.arange(nb)[:, None] * BLOCK + jnp.arange(BLOCK)[None, :]
    kpos = jnp.arange(nb)[:, None] * BLOCK - BLOCK + jnp.arange(3 * BLOCK)[None, :]
    valid = ((jnp.abs(qpos[:, :, None] - kpos[:, None, :]) <= WINDOW)
             & (kpos[:, None, :] >= 0) & (kpos[:, None, :] < L))
    s_loc = jnp.where(valid[None, :, None, None], s_loc, MASK_VALUE)
    s_ctx = jnp.einsum("bnqhgd,bkhd->bnhgqk", qb, kc).astype(jnp.float32) * scale
    p = sink_softmax(jnp.concatenate([s_loc, s_ctx], axis=-1), sink).astype(v.dtype)
    o = (jnp.einsum("bnhgqk,bnkhd->bnqhgd", p[..., :3 * BLOCK], vw)
         + jnp.einsum("bnhgqk,bkhd->bnqhgd", p[..., 3 * BLOCK:], vc))
    return o.reshape(b, L, AT_DIM)


def short_conv_branch(parts, cv_w):
    bg, cg, xin, g = parts
    u = jnp.pad(cg * xin, ((0, 0), (1, 1), (0, 0)))
    conv = u[:, :-2] * cv_w[0] + u[:, 1:-1] * cv_w[1] + u[:, 2:] * cv_w[2]
    return bg * conv * jax.nn.silu(g)


def hybrid_layer(x, xc, c, c_ctx, w_ada, b_ada, norm_g, w_in, w_out, rw_w0, rw_w_up, rw_a0,
                 rw_a_up, rw_k_k, rw_k_a, rw_r_k, rw_ln_g, rw_ln_b, at_sink, cv_w, cos, sin,
                 update_ctx):
    shift, scale, gate = jnp.split((jax.nn.silu(c) @ w_ada + b_ada)[:, None, :], 3, axis=-1)
    shift_c, scale_c, gate_c = jnp.split(jax.nn.silu(c_ctx) @ w_ada + b_ada, 3, axis=-1)
    h = rmsnorm(x, norm_g) * (1 + scale) + shift
    hc = rmsnorm(xc, norm_g) * (1 + scale_c) + shift_c
    offs = [int(o) for o in np.cumsum(IN_SPLITS)[:-1]]
    P = jnp.split(h @ w_in, offs, axis=-1)
    Pc = jnp.split(hc @ w_in, offs, axis=-1)
    rw_params = (rw_w0, rw_w_up, rw_a0, rw_a_up, rw_k_k, rw_k_a, rw_r_k, rw_ln_g, rw_ln_b)
    zeros = jnp.zeros((x.shape[0], RW_HEADS, HEAD_DIM, HEAD_DIM), jnp.float32)
    y_rw_c, s_ctx = rwkv7_branch(Pc[:6], *rw_params, (zeros, zeros))
    y_rw, _ = rwkv7_branch(P[:6], *rw_params, s_ctx)
    q = apply_rope(to_heads(P[6]), cos, sin)
    k = apply_rope(to_heads(P[7]), cos, sin)
    v = to_heads(P[8])
    qc, kc, vc = to_heads(Pc[6]), to_heads(Pc[7]), to_heads(Pc[8])
    y_at = window_attention(q, k, v, kc, vc, at_sink) * jax.nn.silu(P[9])
    y_cv = short_conv_branch(P[10:14], cv_w)
    x = x + gate * (jnp.concatenate([y_rw, y_at, y_cv], axis=-1) @ w_out)
    if update_ctx:
        y_at_c = context_attention(qc, kc, vc, at_sink) * jax.nn.silu(Pc[9])
        y_cv_c = short_conv_branch(Pc[10:14], cv_w)
        xc = xc + gate_c * (jnp.concatenate([y_rw_c, y_at_c, y_cv_c], axis=-1) @ w_out)
    return x, xc


def setup_inputs(seed: int = 0) -> dict:
    key = jax.random.key(seed)
    ks = jax.random.split(key, 24)
    f32 = jnp.float32
    n = jax.random.normal
    return {
        "x": n(ks[0], (BATCH, SEQ, D_MODEL), f32),
        "c": n(ks[1], (BATCH, D_MODEL), f32),
        "ctx": n(ks[2], (BATCH, CTX_LEN, D_MODEL), f32),
        "c_ctx": n(ks[3], (D_MODEL,), f32),
        "w_ada": n(ks[4], (DEPTH, D_MODEL, 3 * D_MODEL), f32) * (0.5 * D_MODEL ** -0.5),
        "b_ada": 0.02 * n(ks[5], (DEPTH, 3 * D_MODEL), f32),
        "norm_g": 1.0 + 0.02 * n(ks[6], (DEPTH, D_MODEL), f32),
        "w_in": n(ks[7], (DEPTH, D_MODEL, D_IN), f32) * D_MODEL ** -0.5,
        "w_out": n(ks[8], (DEPTH, D_MIX, D_MODEL), f32) * D_MIX ** -0.5,
        "rw_w0": jax.random.uniform(ks[9], (DEPTH, 2, RW_DIM), f32, -6.0, -1.0),
        "rw_w_up": n(ks[10], (DEPTH, 2, RW_LORA_W, RW_DIM), f32) * (0.5 * RW_LORA_W ** -0.5),
        "rw_a0": 0.1 * n(ks[11], (DEPTH, 2, RW_DIM), f32),
        "rw_a_up": n(ks[12], (DEPTH, 2, RW_LORA_A, RW_DIM), f32) * RW_LORA_A ** -0.5,
        "rw_k_k": 0.85 + 0.02 * n(ks[13], (DEPTH, RW_DIM), f32),
        "rw_k_a": 1.0 + 0.02 * n(ks[14], (DEPTH, RW_DIM), f32),
        "rw_r_k": 0.1 * n(ks[15], (DEPTH, RW_HEADS, HEAD_DIM), f32),
        "rw_ln_g": 1.0 + 0.02 * n(ks[16], (DEPTH, RW_DIM), f32),
        "rw_ln_b": 0.02 * n(ks[17], (DEPTH, RW_DIM), f32),
        "at_sink": 0.5 * n(ks[18], (DEPTH, AT_HEADS), f32),
        "cv_w": n(ks[19], (DEPTH, CONV_W, CV_DIM), f32) * CONV_W ** -0.5,
        "final_g": 1.0 + 0.02 * n(ks[20], (D_MODEL,), f32),
    }


def reference(x, c, ctx, c_ctx, w_ada, b_ada, norm_g, w_in, w_out, rw_w0, rw_w_up, rw_a0,
              rw_a_up, rw_k_k, rw_k_a, rw_r_k, rw_ln_g, rw_ln_b, at_sink, cv_w, final_g):
    L = x.shape[1]
    ROWS = L // GRID_W
    row = jnp.repeat(jnp.arange(ROWS), GRID_W)
    col = jnp.tile(jnp.arange(GRID_W), ROWS)
    cos, sin = axial_rope_tables(row, col)
    xc = ctx
    for l in range(DEPTH):
        x, xc = hybrid_layer(x, xc, c, c_ctx, w_ada[l], b_ada[l], norm_g[l], w_in[l], w_out[l],
                             rw_w0[l], rw_w_up[l], rw_a0[l], rw_a_up[l], rw_k_k[l], rw_k_a[l],
                             rw_r_k[l], rw_ln_g[l], rw_ln_b[l], at_sink[l], cv_w[l], cos, sin,
                             update_ctx=(l < DEPTH - 1))
    return rmsnorm(x, final_g)
```

```python
import functools

import jax
import jax.numpy as jnp
from jax import lax
from jax.experimental import pallas as pl
from jax.experimental.pallas import tpu as pltpu

F32 = jnp.float32
BF16 = jnp.bfloat16

D_MODEL = 1024
HEAD_DIM = 64
GRID_W = 64
ROPE_THETA = 10000.0
RW_DIM = 384
AT_DIM = 384
CV_DIM = 256
WINDOW = 128
NORM_EPS = 1e-6
LNX_EPS = 64e-5
MASK_VALUE = -1e30
DECAY_SCALE = 0.6065306597126334

COL_RW = (0, 1792)
COL_Q = (1792, 2176)
COL_KV = (2176, 2432)
COL_CV = (2432, 3840)
D_IN = 3840

TM = 256
CHUNK = 64
QB = 128
LANE = 128
VMEM_LIMIT = 56 * 1024 * 1024


def _dot(a, b):
    return jnp.dot(a.astype(BF16), b.astype(BF16), preferred_element_type=F32)


def _dot_nt(a, b):
    return lax.dot_general(a.astype(BF16), b.astype(BF16), (((1,), (1,)), ((), ())),
                           preferred_element_type=F32)


def _dot_hi(a, b):
    return jnp.dot(a, b, preferred_element_type=F32, precision=lax.Precision.HIGHEST)


def _dot_split(x, m, passes=2):
    acc = None
    rem = x
    for p in range(passes):
        hi = rem.astype(BF16)
        t = jnp.dot(hi, m, preferred_element_type=F32)
        acc = t if acc is None else acc + t
        if p + 1 < passes:
            rem = rem - hi.astype(F32)
    return acc


def _sigmoid(x):
    return 1.0 / (1.0 + jnp.exp(-x))


def _head_ones(n, scale=1.0):
    r = lax.broadcasted_iota(jnp.int32, (n, n), 0) // HEAD_DIM
    c = lax.broadcasted_iota(jnp.int32, (n, n), 1) // HEAD_DIM
    return jnp.where(r == c, scale, 0.0).astype(BF16)


def _ada_kernel(c_ref, w_ref, b_ref, o_ref):
    c = c_ref[...]
    o_ref[...] = _dot_hi(c * _sigmoid(c), w_ref[...]) + b_ref[...]


def _ada(cvec, w_ada, b_ada):
    depth = w_ada.shape[0]
    tn = 1024
    return pl.pallas_call(
        _ada_kernel,
        grid=(depth, 3 * D_MODEL // tn),
        in_specs=[
            pl.BlockSpec((8, D_MODEL), lambda l, n: (0, 0)),
            pl.BlockSpec((None, D_MODEL, tn), lambda l, n: (l, 0, n)),
            pl.BlockSpec((None, 1, tn), lambda l, n: (l, 0, n)),
        ],
        out_specs=pl.BlockSpec((None, 8, tn), lambda l, n: (l, 0, n)),
        out_shape=jax.ShapeDtypeStruct((depth, 8, 3 * D_MODEL), F32),
        compiler_params=pltpu.CompilerParams(vmem_limit_bytes=VMEM_LIMIT),
        name="ada",
    )(cvec, w_ada, b_ada.reshape(depth, 1, 3 * D_MODEL))


def _proj_kernel(x_ref, mod_ref, g_ref, w_ref, cos_ref, sin_ref, rw_ref, q_ref, kv_ref, cv_ref):
    x = x_ref[...]
    ms = jnp.mean(x * x, axis=-1, keepdims=True)
    y = x * lax.rsqrt(ms + NORM_EPS) * g_ref[...]
    mod = mod_ref[...]
    shift = mod[:, 0:D_MODEL]
    scale = mod[:, D_MODEL:2 * D_MODEL]
    h = (y * (1.0 + scale) + shift).astype(BF16)

    def seg(cols):
        return jnp.dot(h, w_ref[:, cols[0]:cols[1]], preferred_element_type=F32)

    rw_ref[...] = seg(COL_RW)
    cv_ref[...] = seg(COL_CV)

    cos = cos_ref[...]
    sin = sin_ref[...]
    lane = lax.broadcasted_iota(jnp.int32, (TM, LANE), 1)
    first_half = (lane % HEAD_DIM) < (HEAD_DIM // 2)
    low_head = lane < HEAD_DIM

    def rope(t):
        rot = jnp.where(first_half, pltpu.roll(t, LANE - 32, 1), pltpu.roll(t, 32, 1))
        return t * cos + rot * sin

    def dup(t):
        sw = pltpu.roll(t, HEAD_DIM, 1)
        return jnp.where(low_head, t, sw), jnp.where(low_head, sw, t)

    q = seg(COL_Q)
    q_ref[...] = jnp.concatenate(
        [rope(q[:, LANE * j:LANE * (j + 1)]) * (HEAD_DIM ** -0.5) for j in range(AT_DIM // LANE)], axis=1)
    kv = seg(COL_KV)
    k0, k1 = dup(rope(kv[:, 0:LANE]))
    v0, v1 = dup(kv[:, LANE:2 * LANE])
    kv_ref[...] = jnp.concatenate([k0, k1, v0, v1], axis=1)


def _proj(xa, mod, norm_g, w_in, cos_t, sin_t, nct):
    b, r, _ = xa.shape
    nt = r // TM
    row = lambda bi, i: (bi, i, 0)
    return pl.pallas_call(
        _proj_kernel,
        grid=(b, nt),
        in_specs=[
            pl.BlockSpec((None, TM, D_MODEL), row),
            pl.BlockSpec((None, 1, 3 * D_MODEL), lambda bi, i: (jnp.where(i < nct, b, bi), 0, 0)),
            pl.BlockSpec((1, D_MODEL), lambda bi, i: (0, 0)),
            pl.BlockSpec((D_MODEL, D_IN), lambda bi, i: (0, 0)),
            pl.BlockSpec((TM, LANE), lambda bi, i: (i, 0)),
            pl.BlockSpec((TM, LANE), lambda bi, i: (i, 0)),
        ],
        out_specs=[
            pl.BlockSpec((None, TM, COL_RW[1] - COL_RW[0]), row),
            pl.BlockSpec((None, TM, AT_DIM), row),
            pl.BlockSpec((None, TM, 4 * LANE), row),
            pl.BlockSpec((None, TM, COL_CV[1] - COL_CV[0]), row),
        ],
        out_shape=[
            jax.ShapeDtypeStruct((b, r, COL_RW[1] - COL_RW[0]), F32),
            jax.ShapeDtypeStruct((b, r, AT_DIM), F32),
            jax.ShapeDtypeStruct((b, r, 4 * LANE), F32),
            jax.ShapeDtypeStruct((b, r, COL_CV[1] - COL_CV[0]), F32),
        ],
        compiler_params=pltpu.CompilerParams(
            dimension_semantics=("parallel", "parallel"), vmem_limit_bytes=VMEM_LIMIT),
        name="proj",
    )(xa, mod, norm_g, w_in, cos_t, sin_t)


_RW_NAMES = ("At", "Rt", "Kt", "Bt", "Kh", "Bh", "Vv", "Pe")
_RW_GROUPS = ((0, 256), (256, 384))


def _group_masks(w, rev):
    nh = w // HEAD_DIM
    row = lax.broadcasted_iota(jnp.int32, (CHUNK, w), 0)
    lane = lax.broadcasted_iota(jnp.int32, (CHUNK, w), 1)
    s = lane % CHUNK
    x = row ^ s
    m = {
        "strict": (s > row) if rev else (s < row),
        "incl": (s >= row) if rev else (s <= row),
        "eye": s == row,
        "lvl1": (x >> 1) == 0,
        "heads": [(lane // HEAD_DIM) == h for h in range(nh)],
    }
    for lg in range(1, 6):
        m["lvl%d" % (1 << lg)] = (x >> lg) == 1
    rw = lax.broadcasted_iota(jnp.int32, (w, w), 0)
    cw = lax.broadcasted_iota(jnp.int32, (w, w), 1)
    m["bd"] = (rw // HEAD_DIM) == (cw // HEAD_DIM)
    m["eyew"] = rw == cw
    return m


def _chunk_group(at, rt, kt, bt, kh, bh, v, pe, h_ref, m):
    def bd(x):
        return jnp.concatenate([jnp.where(hm, x, 0.0) for hm in m["heads"]], axis=0)

    ar = jnp.concatenate([at, rt], axis=0)
    gb = _dot_nt(ar, bd(bt))
    gk = _dot_nt(ar, bd(kt))
    lo = jnp.where(m["strict"], gb[:CHUNK], 0.0)
    aak = jnp.where(m["strict"], gk[:CHUNK], 0.0)
    arb = jnp.where(m["incl"], gb[CHUNK:], 0.0)
    ark = jnp.where(m["incl"], gk[CHUNK:], 0.0)

    t = jnp.where(m["eye"], 1.0, 0.0) - jnp.where(m["lvl1"], lo, 0.0)
    for n in (2, 4, 8, 16, 32):
        cm = jnp.where(m["lvl%d" % n], lo, 0.0)
        t = t - _dot(_dot(t, bd(cm)), bd(t))

    av = _dot(jnp.concatenate([aak, ark], axis=0), bd(v))
    akv, rkv = av[:CHUNK], av[CHUNK:]
    wt = _dot(t, bd(at))
    ut = _dot(t, bd(akv))
    qh = rt - _dot(arb, bd(wt))
    yi = rkv - _dot(arb, bd(ut))

    bht = bh.T
    mm = jnp.where(m["bd"], jnp.where(m["eyew"], pe, 0.0) - _dot(bht, wt), 0.0)
    nn = jnp.where(m["bd"], _dot(kh.T, v) - _dot(bht, ut), 0.0)
    h0 = h_ref[...]
    y = _dot(qh, h0) + yi
    h_ref[...] = _dot(mm, h0) + nn
    return y


def _rwkv_kernel(xf_ref, xb_ref, w0_ref, wup_ref, a0_ref, aup_ref, kk_ref, ka_ref,
                 yf_ref, yb_ref, *scratch):
    ns = len(_RW_NAMES)
    scr = [dict(zip(_RW_NAMES, scratch[d * ns:(d + 1) * ns])) for d in range(2)]
    hst = [scratch[2 * ns + 2 * d:2 * ns + 2 * d + 2] for d in range(2)]
    nchunk = TM // CHUNK

    @pl.when(pl.program_id(1) == 0)
    def _():
        for d in range(2):
            for h_ref in hst[d]:
                h_ref[...] = jnp.zeros_like(h_ref)

    rt_i = lax.broadcasted_iota(jnp.int32, (TM, TM), 0)
    ct_i = lax.broadcasted_iota(jnp.int32, (TM, TM), 1)
    same_chunk = (rt_i // CHUNK) == (ct_i // CHUNK)
    chunk_ones = jnp.where(same_chunk, 1.0, 0.0)
    ones_h = _head_ones(RW_DIM)
    lane = lax.broadcasted_iota(jnp.int32, (TM, LANE), 1)

    for d, x_ref in enumerate((xf_ref, xb_ref)):
        rev = d == 1
        r = x_ref[:, 0:384]
        k = x_ref[:, 384:768]
        v = x_ref[:, 768:1152]
        dmask = (lane // HEAD_DIM) == d
        tw = jnp.where(dmask, jnp.tanh(x_ref[:, 1536:1664]), 0.0)
        ta = jnp.where(dmask, x_ref[:, 1664:1792], 0.0)
        lw = w0_ref[d:d + 1, :] + _dot_hi(tw, wup_ref[...])
        logw = -DECAY_SCALE * _sigmoid(lw)
        a = _sigmoid(a0_ref[d:d + 1, :] + _dot_hi(ta, aup_ref[...]))
        kk = k * kk_ref[...]
        ss = _dot_split(kk * kk, ones_h, passes=3)
        kk = kk * lax.rsqrt(jnp.maximum(ss, 1e-12))
        kd = k * (1.0 + (a - 1.0) * ka_ref[...])
        beta = kk * a
        tri = jnp.where(same_chunk & ((ct_i >= rt_i) if rev else (ct_i <= rt_i)), 1.0, 0.0)
        cl = _dot_hi(tri, logw)
        pend = _dot_hi(chunk_ones, logw)
        inv = jnp.exp(-cl)
        tail = jnp.exp(pend - cl)
        s = scr[d]
        s["At"][...] = kk * jnp.exp(cl - logw)
        s["Rt"][...] = r * jnp.exp(cl)
        s["Kt"][...] = kd * inv
        s["Bt"][...] = beta * inv
        s["Kh"][...] = kd * tail
        s["Bh"][...] = beta * tail
        s["Vv"][...] = v
        s["Pe"][...] = jnp.exp(pend)

    masks = {(w, rev): _group_masks(w, rev) for w in (256, 128) for rev in (False, True)}

    def body(c, carry):
        for d, y_ref in enumerate((yf_ref, yb_ref)):
            rev = d == 1
            start = pl.multiple_of(((nchunk - 1 - c) if rev else c) * CHUNK, CHUNK)
            rows = pl.ds(start, CHUNK)
            for g, (lo, hi) in enumerate(_RW_GROUPS):
                s = scr[d]
                args = [s[n][rows, lo:hi] for n in _RW_NAMES[:7]]
                pe = s["Pe"][pl.ds(start, 1), lo:hi]
                y_ref[rows, lo:hi] = _chunk_group(*args, pe, hst[d][g], masks[(hi - lo, rev)])
        return carry

    lax.fori_loop(0, nchunk, body, 0)


def _bwd_tile(i, nct, nt):
    return jnp.where(i < nct, nct - 1 - i, nt - 1 - (i - nct))


def _rwkv(p_rw, w0, wup, a0, aup, k_k, k_a, nct):
    b, r, wcols = p_rw.shape
    nt = r // TM
    fwd = lambda bi, i: (bi, i, 0)
    bwd = lambda bi, i: (bi, _bwd_tile(i, nct, nt), 0)
    full = lambda shape: pl.BlockSpec(shape, lambda bi, i: (0,) * len(shape))
    scratch = [pltpu.VMEM((TM, RW_DIM), F32) for _ in range(2 * len(_RW_NAMES))]
    for _ in range(2):
        scratch += [pltpu.VMEM((hi - lo, hi - lo), F32) for lo, hi in _RW_GROUPS]
    return pl.pallas_call(
        _rwkv_kernel,
        grid=(b, nt),
        in_specs=[
            pl.BlockSpec((None, TM, wcols), fwd),
            pl.BlockSpec((None, TM, wcols), bwd),
            full((2, RW_DIM)), full((2 * HEAD_DIM, RW_DIM)),
            full((2, RW_DIM)), full((2 * HEAD_DIM, RW_DIM)),
            full((1, RW_DIM)), full((1, RW_DIM)),
        ],
        out_specs=[pl.BlockSpec((None, TM, RW_DIM), fwd), pl.BlockSpec((None, TM, RW_DIM), bwd)],
        out_shape=[jax.ShapeDtypeStruct((b, r, RW_DIM), F32)] * 2,
        scratch_shapes=scratch,
        compiler_params=pltpu.CompilerParams(
            dimension_semantics=("parallel", "arbitrary"), vmem_limit_bytes=VMEM_LIMIT),
        name="rwkv",
    )(p_rw, p_rw, w0, wup, a0, aup, k_k, k_a)


def _attn_kernel(sink_ref, q_ref, kp_ref, kc_ref, kn_ref, kx_ref, o_ref, *, ncb, nb):
    j = pl.program_id(1)
    latent = j >= ncb
    has_prev = (j - 1) >= ncb
    has_next = (j + 1) <= nb - 1
    nloc = 3 * QB
    nkeys = nloc + kx_ref.shape[0]

    lane = lax.broadcasted_iota(jnp.int32, (QB, LANE), 1)
    low = lane < HEAD_DIM
    qt = [q_ref[:, LANE * t:LANE * (t + 1)] for t in range(AT_DIM // LANE)]

    row = lax.broadcasted_iota(jnp.int32, (3 * QB, nkeys), 0)
    col = lax.broadcasted_iota(jnp.int32, (3 * QB, nkeys), 1)
    tq = row % QB
    lo_b = jnp.maximum(tq, jnp.where(has_prev, 0, QB))
    hi_b = jnp.minimum(tq + 2 * WINDOW, jnp.where(has_next, nloc - 1, 2 * QB - 1))
    hi_b = jnp.where(latent, hi_b, -1)
    valid = ((col >= lo_b) & (col <= hi_b)) | (col >= nloc)
    hrow = lax.broadcasted_iota(jnp.int32, (3 * QB, 1), 0) // QB

    outs = []
    for g in range(2):
        if g == 0:
            qs = [jnp.where(low, qt[0], 0.0), jnp.where(low, 0.0, qt[0]), jnp.where(low, qt[1], 0.0)]
        else:
            qs = [jnp.where(low, 0.0, qt[1]), jnp.where(low, qt[2], 0.0), jnp.where(low, 0.0, qt[2])]
        qs = jnp.concatenate(qs, axis=0)
        kcols = slice(LANE * g, LANE * (g + 1))
        vcols = slice(2 * LANE + LANE * g, 2 * LANE + LANE * (g + 1))
        kall = jnp.concatenate([kp_ref[:, kcols], kc_ref[:, kcols], kn_ref[:, kcols], kx_ref[:, kcols]], axis=0)
        vall = jnp.concatenate([kp_ref[:, vcols], kc_ref[:, vcols], kn_ref[:, vcols], kx_ref[:, vcols]], axis=0)
        s = jnp.where(valid, _dot_nt(qs, kall), MASK_VALUE)
        sk = jnp.where(hrow == 0, sink_ref[3 * g], jnp.where(hrow == 1, sink_ref[3 * g + 1], sink_ref[3 * g + 2]))
        mx = jnp.maximum(jnp.max(s, axis=-1, keepdims=True), sk)
        p = jnp.exp(s - mx)
        den = jnp.sum(p, axis=-1, keepdims=True) + jnp.exp(sk - mx)
        outs.append(_dot(p, vall) / den)

    o0, o1 = outs
    o_ref[...] = jnp.concatenate([
        jnp.where(low, o0[0:QB], o0[QB:2 * QB]),
        jnp.where(low, o0[2 * QB:3 * QB], o1[0:QB]),
        jnp.where(low, o1[QB:2 * QB], o1[2 * QB:3 * QB]),
    ], axis=1)


def _attn(p_q, p_kv, sink, ctx_len):
    b, r, _ = p_q.shape
    nb = r // QB
    ncb = ctx_len // QB
    clamp = lambda j: jnp.clip(j, ncb, nb - 1)
    kvw = p_kv.shape[-1]
    return pl.pallas_call(
        functools.partial(_attn_kernel, ncb=ncb, nb=nb),
        grid=(b, nb),
        in_specs=[
            pl.BlockSpec(memory_space=pltpu.SMEM),
            pl.BlockSpec((None, QB, AT_DIM), lambda bi, j: (bi, j, 0)),
            pl.BlockSpec((None, QB, kvw), lambda bi, j: (bi, clamp(j - 1), 0)),
            pl.BlockSpec((None, QB, kvw), lambda bi, j: (bi, clamp(j), 0)),
            pl.BlockSpec((None, QB, kvw), lambda bi, j: (bi, clamp(j + 1), 0)),
            pl.BlockSpec((None, ctx_len, kvw), lambda bi, j: (bi, 0, 0)),
        ],
        out_specs=pl.BlockSpec((None, QB, AT_DIM), lambda bi, j: (bi, j, 0)),
        out_shape=jax.ShapeDtypeStruct((b, r, AT_DIM), F32),
        compiler_params=pltpu.CompilerParams(
            dimension_semantics=("parallel", "parallel"), vmem_limit_bytes=VMEM_LIMIT),
        name="attn",
    )(sink, p_q, p_kv, p_kv, p_kv, p_kv)


def _out_kernel(x_ref, mod_ref, rw_ref, yf_ref, yb_ref, at_ref, cv_ref, cvp_ref, cvn_ref,
                rk_ref, lng_ref, lnb_ref, cvw_ref, wout_ref, fg_ref, o_ref, *, tile_off, ctx_len, rows, final):
    i = pl.program_id(1) + tile_off
    row0 = i * TM
    r = rw_ref[:, 0:384]
    k = rw_ref[:, 384:768]
    v = rw_ref[:, 768:1152]
    g = rw_ref[:, 1152:1536]

    mean_h = _head_ones(RW_DIM, 1.0 / HEAD_DIM)
    y = yf_ref[...] + yb_ref[...]
    dev = y - _dot_split(y, mean_h)
    var = _dot_split(dev * dev, mean_h)
    yn = dev * lax.rsqrt(var + LNX_EPS) * lng_ref[...] + lnb_ref[...]
    bonus = _dot_split(r * k * rk_ref[...], _head_ones(RW_DIM)) * v
    y_rw = (yn + bonus) * (g * _sigmoid(g))

    g_at = cv_ref[:, 0:384]
    y_at = at_ref[...] * (g_at * _sigmoid(g_at))

    bg = cv_ref[:, 384:640]
    u = cv_ref[:, 640:896] * cv_ref[:, 896:1152]
    g_cv = cv_ref[:, 1152:1408]
    prev_ok = jnp.logical_and(row0 != 0, row0 != ctx_len)
    next_ok = jnp.logical_and(row0 + TM != ctx_len, row0 + TM != rows)
    u_prev = jnp.where(prev_ok, cvp_ref[7:8, 640:896] * cvp_ref[7:8, 896:1152], 0.0)
    u_next = jnp.where(next_ok, cvn_ref[0:1, 640:896] * cvn_ref[0:1, 896:1152], 0.0)
    ridx = lax.broadcasted_iota(jnp.int32, (TM, CV_DIM), 0)
    um = jnp.where(ridx == 0, u_prev, pltpu.roll(u, 1, 0))
    up = jnp.where(ridx == TM - 1, u_next, pltpu.roll(u, TM - 1, 0))
    conv = um * cvw_ref[0:1, :] + u * cvw_ref[1:2, :] + up * cvw_ref[2:3, :]
    y_cv = bg * conv * (g_cv * _sigmoid(g_cv))

    acc = (jnp.dot(y_rw.astype(BF16), wout_ref[0:384, :], preferred_element_type=F32)
           + jnp.dot(y_at.astype(BF16), wout_ref[384:768, :], preferred_element_type=F32)
           + jnp.dot(y_cv.astype(BF16), wout_ref[768:1024, :], preferred_element_type=F32))
    gate = mod_ref[:, 2 * D_MODEL:3 * D_MODEL]
    xn = x_ref[...] + gate * acc
    if final:
        ms = jnp.mean(xn * xn, axis=-1, keepdims=True)
        xn = xn * lax.rsqrt(ms + NORM_EPS) * fg_ref[...]
    o_ref[...] = xn


def _out(xa, mod, p_rw, y_f, y_b, y_at, p_cv, r_k, ln_g, ln_b, cv_w, w_out, final_g, nct, ctx_len, final):
    b, r, _ = xa.shape
    nt = r // TM
    off = nct if final else 0
    cvw = p_cv.shape[-1]
    hb = TM // 8
    row = lambda bi, i: (bi, i + off, 0)
    full = lambda shape: pl.BlockSpec(shape, lambda bi, i: (0,) * len(shape))
    out_rows = r - off * TM
    return pl.pallas_call(
        functools.partial(_out_kernel, tile_off=off, ctx_len=ctx_len, rows=r, final=final),
        grid=(b, nt - off),
        in_specs=[
            pl.BlockSpec((None, TM, D_MODEL), row),
            pl.BlockSpec((None, 1, 3 * D_MODEL), lambda bi, i: (jnp.where(i + off < nct, b, bi), 0, 0)),
            pl.BlockSpec((None, TM, 1536), row),
            pl.BlockSpec((None, TM, RW_DIM), row),
            pl.BlockSpec((None, TM, RW_DIM), row),
            pl.BlockSpec((None, TM, AT_DIM), row),
            pl.BlockSpec((None, TM, cvw), row),
            pl.BlockSpec((None, 8, cvw), lambda bi, i: (bi, jnp.maximum((i + off) * hb - 1, 0), 0)),
            pl.BlockSpec((None, 8, cvw), lambda bi, i: (bi, jnp.minimum((i + off + 1) * hb, r // 8 - 1), 0)),
            full((1, RW_DIM)), full((1, RW_DIM)), full((1, RW_DIM)),
            full((3, CV_DIM)), full((D_MODEL, D_MODEL)), full((1, D_MODEL)),
        ],
        out_specs=pl.BlockSpec((None, TM, D_MODEL), lambda bi, i: (bi, i, 0)),
        out_shape=jax.ShapeDtypeStruct((b, out_rows, D_MODEL), F32),
        compiler_params=pltpu.CompilerParams(
            dimension_semantics=("parallel", "parallel"), vmem_limit_bytes=VMEM_LIMIT),
        name="out",
    )(xa, mod, p_rw, y_f, y_b, y_at, p_cv, p_cv, p_cv, r_k, ln_g, ln_b, cv_w, w_out, final_g)


def _rope_tables(seq, ctx_len):
    t = jnp.arange(seq)
    nfreq = HEAD_DIM // 4
    inv = ROPE_THETA ** (-jnp.arange(nfreq, dtype=F32) / nfreq)
    ang = jnp.concatenate([(t // GRID_W).astype(F32)[:, None] * inv,
                           (t % GRID_W).astype(F32)[:, None] * inv], axis=-1)
    cos, sin = jnp.cos(ang), jnp.sin(ang)
    reps = LANE // HEAD_DIM
    cos_t = jnp.tile(jnp.concatenate([cos, cos], axis=-1), (1, reps))
    sin_t = jnp.tile(jnp.concatenate([-sin, sin], axis=-1), (1, reps))
    cos_t = jnp.concatenate([jnp.ones((ctx_len, LANE), F32), cos_t], axis=0)
    sin_t = jnp.concatenate([jnp.zeros((ctx_len, LANE), F32), sin_t], axis=0)
    return cos_t, sin_t


def kernel(x, c, ctx, c_ctx, w_ada, b_ada, norm_g, w_in, w_out, rw_w0, rw_w_up, rw_a0, rw_a_up,
           rw_k_k, rw_k_a, rw_r_k, rw_ln_g, rw_ln_b, at_sink, cv_w, final_g):
    b, seq, _ = x.shape
    ctx_len = ctx.shape[1]
    depth = w_in.shape[0]
    assert ctx_len % TM == 0 and seq % TM == 0 and b + 1 <= 8
    nct = ctx_len // TM

    xa = jnp.concatenate([ctx, x], axis=1)
    cvec = jnp.zeros((8, D_MODEL), F32).at[0:b].set(c).at[b].set(c_ctx)
    mod = _ada(cvec, w_ada, b_ada)
    cos_t, sin_t = _rope_tables(seq, ctx_len)
    w_in_b = w_in.astype(BF16)
    w_out_b = w_out.astype(BF16)

    for l in range(depth):
        mod_l = mod[l].reshape(8, 1, 3 * D_MODEL)
        p_rw, p_q, p_kv, p_cv = _proj(xa, mod_l, norm_g[l].reshape(1, -1), w_in_b[l], cos_t, sin_t, nct)
        y_f, y_b = _rwkv(p_rw, rw_w0[l], rw_w_up[l].reshape(2 * HEAD_DIM, RW_DIM), rw_a0[l],
                         rw_a_up[l].reshape(2 * HEAD_DIM, RW_DIM), rw_k_k[l].reshape(1, -1),
                         rw_k_a[l].reshape(1, -1), nct)
        y_at = _attn(p_q, p_kv, at_sink[l], ctx_len)
        xa = _out(xa, mod_l, p_rw, y_f, y_b, y_at, p_cv, rw_r_k[l].reshape(1, -1),
                  rw_ln_g[l].reshape(1, -1), rw_ln_b[l].reshape(1, -1), cv_w[l], w_out_b[l],
                  final_g.reshape(1, -1), nct, ctx_len, final=(l == depth - 1))
    return xa
```

```python
import functools

import jax
import jax.numpy as jnp
from jax import lax
from jax.experimental import pallas as pl
from jax.experimental.pallas import tpu as pltpu

F32 = jnp.float32
BF16 = jnp.bfloat16

D_MODEL = 1024
HEAD_DIM = 64
GRID_W = 64
ROPE_THETA = 10000.0
RW_DIM = 384
AT_DIM = 384
CV_DIM = 256
WINDOW = 128
NORM_EPS = 1e-6
LNX_EPS = 64e-5
MASK_VALUE = -1e30
DECAY_SCALE = 0.6065306597126334

COL_RW = (0, 1792)
COL_Q = (1792, 2176)
COL_KV = (2176, 2432)
COL_CV = (2432, 3840)
D_IN = 3840

TM = 256
CHUNK = 64
QB = 128
LANE = 128
VMEM_LIMIT = 56 * 1024 * 1024


def _dot(a, b):
    return jnp.dot(a.astype(BF16), b.astype(BF16), preferred_element_type=F32)


def _dot_nt(a, b):
    return lax.dot_general(a.astype(BF16), b.astype(BF16), (((1,), (1,)), ((), ())),
                           preferred_element_type=F32)


def _dot_hi(a, b):
    return jnp.dot(a, b, preferred_element_type=F32, precision=lax.Precision.HIGHEST)


def _dot_split(a, b, passes=2, left_exact=False):
    acc = None
    rem = a if not left_exact else b
    for p in range(passes):
        hi = rem.astype(BF16)
        t = jnp.dot(a, hi, preferred_element_type=F32) if left_exact else jnp.dot(hi, b, preferred_element_type=F32)
        acc = t if acc is None else acc + t
        if p + 1 < passes:
            rem = rem - hi.astype(F32)
    return acc


def _sigmoid(x):
    return 1.0 / (1.0 + jnp.exp(-x))


def _head_ones(n, scale=1.0):
    r = lax.broadcasted_iota(jnp.int32, (n, n), 0) // HEAD_DIM
    c = lax.broadcasted_iota(jnp.int32, (n, n), 1) // HEAD_DIM
    return jnp.where(r == c, scale, 0.0).astype(BF16)


def _ada_kernel(c_ref, w_ref, b_ref, o_ref):
    c = c_ref[...]
    o_ref[...] = _dot_hi(c * _sigmoid(c), w_ref[...]) + b_ref[...]


def _ada(cvec, w_ada, b_ada):
    depth = w_ada.shape[0]
    tn = 1024
    return pl.pallas_call(
        _ada_kernel,
        grid=(depth, 3 * D_MODEL // tn),
        in_specs=[
            pl.BlockSpec((8, D_MODEL), lambda l, n: (0, 0)),
            pl.BlockSpec((None, D_MODEL, tn), lambda l, n: (l, 0, n)),
            pl.BlockSpec((None, 1, tn), lambda l, n: (l, 0, n)),
        ],
        out_specs=pl.BlockSpec((None, 8, tn), lambda l, n: (l, 0, n)),
        out_shape=jax.ShapeDtypeStruct((depth, 8, 3 * D_MODEL), F32),
        compiler_params=pltpu.CompilerParams(vmem_limit_bytes=VMEM_LIMIT),
        name="ada",
    )(cvec, w_ada, b_ada.reshape(depth, 1, 3 * D_MODEL))


def _proj_kernel(x_ref, mod_ref, g_ref, w_ref, cos_ref, sin_ref, rw_ref, q_ref, kv_ref, cv_ref):
    x = x_ref[...]
    ms = jnp.mean(x * x, axis=-1, keepdims=True)
    y = x * lax.rsqrt(ms + NORM_EPS) * g_ref[...]
    mod = mod_ref[...]
    shift = mod[:, 0:D_MODEL]
    scale = mod[:, D_MODEL:2 * D_MODEL]
    h = (y * (1.0 + scale) + shift).astype(BF16)

    def seg(cols):
        return jnp.dot(h, w_ref[:, cols[0]:cols[1]], preferred_element_type=F32)

    rw_ref[...] = seg(COL_RW)
    cv_ref[...] = seg(COL_CV)

    cos = cos_ref[...]
    sin = sin_ref[...]
    lane = lax.broadcasted_iota(jnp.int32, (TM, LANE), 1)
    first_half = (lane % HEAD_DIM) < (HEAD_DIM // 2)
    low_head = lane < HEAD_DIM

    def rope(t):
        rot = jnp.where(first_half, pltpu.roll(t, LANE - 32, 1), pltpu.roll(t, 32, 1))
        return t * cos + rot * sin

    def dup(t):
        sw = pltpu.roll(t, HEAD_DIM, 1)
        return jnp.where(low_head, t, sw), jnp.where(low_head, sw, t)

    q = seg(COL_Q)
    q_ref[...] = jnp.concatenate(
        [rope(q[:, LANE * j:LANE * (j + 1)]) * (HEAD_DIM ** -0.5) for j in range(AT_DIM // LANE)], axis=1)
    kv = seg(COL_KV)
    k0, k1 = dup(rope(kv[:, 0:LANE]))
    v0, v1 = dup(kv[:, LANE:2 * LANE])
    kv_ref[...] = jnp.concatenate([k0, k1, v0, v1], axis=1)


def _proj(xa, mod, norm_g, w_in, cos_t, sin_t, nct):
    b, r, _ = xa.shape
    nt = r // TM
    row = lambda bi, i: (bi, i, 0)
    return pl.pallas_call(
        _proj_kernel,
        grid=(b, nt),
        in_specs=[
            pl.BlockSpec((None, TM, D_MODEL), row),
            pl.BlockSpec((None, 1, 3 * D_MODEL), lambda bi, i: (jnp.where(i < nct, b, bi), 0, 0)),
            pl.BlockSpec((1, D_MODEL), lambda bi, i: (0, 0)),
            pl.BlockSpec((D_MODEL, D_IN), lambda bi, i: (0, 0)),
            pl.BlockSpec((TM, LANE), lambda bi, i: (i, 0)),
            pl.BlockSpec((TM, LANE), lambda bi, i: (i, 0)),
        ],
        out_specs=[
            pl.BlockSpec((None, TM, COL_RW[1] - COL_RW[0]), row),
            pl.BlockSpec((None, TM, AT_DIM), row),
            pl.BlockSpec((None, TM, 4 * LANE), row),
            pl.BlockSpec((None, TM, COL_CV[1] - COL_CV[0]), row),
        ],
        out_shape=[
            jax.ShapeDtypeStruct((b, r, COL_RW[1] - COL_RW[0]), F32),
            jax.ShapeDtypeStruct((b, r, AT_DIM), F32),
            jax.ShapeDtypeStruct((b, r, 4 * LANE), F32),
            jax.ShapeDtypeStruct((b, r, COL_CV[1] - COL_CV[0]), F32),
        ],
        compiler_params=pltpu.CompilerParams(
            dimension_semantics=("parallel", "parallel"), vmem_limit_bytes=VMEM_LIMIT),
        name="proj",
    )(xa, mod, norm_g, w_in, cos_t, sin_t)


_RW_GROUPS = ((0, 256), (256, 384))


def _group_masks(w, rev):
    nh = w // HEAD_DIM
    row = lax.broadcasted_iota(jnp.int32, (CHUNK, w), 0)
    lane = lax.broadcasted_iota(jnp.int32, (CHUNK, w), 1)
    s = lane % CHUNK
    x = row ^ s
    m = {
        "strict": (s > row) if rev else (s < row),
        "incl": (s >= row) if rev else (s <= row),
        "eye": s == row,
        "lvl1": (x >> 1) == 0,
        "heads": [jnp.where((lane // HEAD_DIM) == h, 1.0, 0.0).astype(BF16) for h in range(nh)],
    }
    for lg in range(1, 6):
        m["lvl%d" % (1 << lg)] = (x >> lg) == 1
    rw = lax.broadcasted_iota(jnp.int32, (w, w), 0)
    cw = lax.broadcasted_iota(jnp.int32, (w, w), 1)
    m["bd"] = (rw // HEAD_DIM) == (cw // HEAD_DIM)
    m["eyew"] = rw == cw
    return m


def _bd(x, m):
    xb = x.astype(BF16)
    return jnp.concatenate([xb * hm for hm in m["heads"]], axis=0)


def _mm(a, b):
    return jnp.dot(a.astype(BF16), b, preferred_element_type=F32)


def _mm_nt(a, b):
    return lax.dot_general(a.astype(BF16), b, (((1,), (1,)), ((), ())), preferred_element_type=F32)


def _chunk_pieces(ch):
    n = range(len(ch))
    ms = [c["m"] for c in ch]
    ar = [jnp.concatenate([c["at"], c["rt"]], axis=0) for c in ch]
    gb = [_mm_nt(ar[i], _bd(ch[i]["bt"], ms[i])) for i in n]
    gk = [_mm_nt(ar[i], _bd(ch[i]["kt"], ms[i])) for i in n]
    lo = [jnp.where(ms[i]["strict"], gb[i][:CHUNK], 0.0) for i in n]
    aak = [jnp.where(ms[i]["strict"], gk[i][:CHUNK], 0.0) for i in n]
    arb = [jnp.where(ms[i]["incl"], gb[i][CHUNK:], 0.0) for i in n]
    ark = [jnp.where(ms[i]["incl"], gk[i][CHUNK:], 0.0) for i in n]

    t = [jnp.where(ms[i]["eye"], 1.0, 0.0) - jnp.where(ms[i]["lvl1"], lo[i], 0.0) for i in n]
    for lvl in (2, 4, 8, 16, 32):
        x = [_mm(t[i], _bd(jnp.where(ms[i]["lvl%d" % lvl], lo[i], 0.0), ms[i])) for i in n]
        t = [t[i] - _mm(x[i], _bd(t[i], ms[i])) for i in n]

    av = [_mm(jnp.concatenate([aak[i], ark[i]], axis=0), _bd(ch[i]["v"], ms[i])) for i in n]
    wt = [_mm(t[i], _bd(ch[i]["at"], ms[i])) for i in n]
    ut = [_mm(t[i], _bd(av[i][:CHUNK], ms[i])) for i in n]
    qh = [ch[i]["rt"] - _mm(arb[i], _bd(wt[i], ms[i])) for i in n]
    yi = [av[i][CHUNK:] - _mm(arb[i], _bd(ut[i], ms[i])) for i in n]
    bht = [c["bh"].T for c in ch]
    bw = [_dot(bht[i], wt[i]) for i in n]
    kv = [_dot(ch[i]["kh"].T, ch[i]["v"]) for i in n]
    bu = [_dot(bht[i], ut[i]) for i in n]
    mm = [jnp.where(ms[i]["bd"], jnp.where(ms[i]["eyew"], ch[i]["pe"], 0.0) - bw[i], 0.0) for i in n]
    nn = [jnp.where(ms[i]["bd"], kv[i] - bu[i], 0.0) for i in n]
    return qh, yi, mm, nn


def _rwkv_kernel(xf_ref, xb_ref, w0_ref, wup_ref, a0_ref, aup_ref, kk_ref, ka_ref,
                 yf_ref, yb_ref, *h_refs):
    nchunk = TM // CHUNK
    ngroup = len(_RW_GROUPS)

    @pl.when(pl.program_id(1) == 0)
    def _():
        for h_ref in h_refs:
            h_ref[...] = jnp.zeros_like(h_ref)

    rt_i = lax.broadcasted_iota(jnp.int32, (TM, TM), 0)
    ct_i = lax.broadcasted_iota(jnp.int32, (TM, TM), 1)
    same_chunk = (rt_i // CHUNK) == (ct_i // CHUNK)
    ones_h = _head_ones(RW_DIM)
    lane = lax.broadcasted_iota(jnp.int32, (TM, LANE), 1)
    wup = wup_ref[...].astype(BF16)
    aup = aup_ref[...].astype(BF16)
    masks = {(w, rev): _group_masks(w, rev) for w in (256, 128) for rev in (False, True)}

    chains = []
    for d, x_ref in enumerate((xf_ref, xb_ref)):
        rev = d == 1
        r = x_ref[:, 0:384]
        k = x_ref[:, 384:768]
        v = x_ref[:, 768:1152]
        dmask = (lane // HEAD_DIM) == d
        tw = jnp.where(dmask, jnp.tanh(x_ref[:, 1536:1664]), 0.0)
        ta = jnp.where(dmask, x_ref[:, 1664:1792], 0.0)
        lw = w0_ref[d:d + 1, :] + _mm(tw, wup)
        logw = -DECAY_SCALE * _sigmoid(lw)
        a = _sigmoid(a0_ref[d:d + 1, :] + _mm(ta, aup))
        kk = k * kk_ref[...]
        ss = _dot_split(kk * kk, ones_h)
        kk = kk * lax.rsqrt(jnp.maximum(ss, 1e-12))
        kd = k * (1.0 + (a - 1.0) * ka_ref[...])
        beta = kk * a
        tri = jnp.where(same_chunk & ((ct_i >= rt_i) if rev else (ct_i <= rt_i)), 1.0, 0.0).astype(BF16)
        cl = _dot_split(tri, logw, left_exact=True)
        ends = [(c * CHUNK) if rev else (c * CHUNK + CHUNK - 1) for c in range(nchunk)]
        pend = jnp.concatenate([jnp.broadcast_to(cl[e:e + 1, :], (CHUNK, RW_DIM)) for e in ends], axis=0)
        inv = jnp.exp(-cl)
        tail = jnp.exp(pend - cl)
        full = dict(at=kk * jnp.exp(cl - logw), rt=r * jnp.exp(cl), kt=kd * inv, bt=beta * inv,
                    kh=kd * tail, bh=beta * tail, v=v)
        pe = jnp.exp(pend)
        for step in range(nchunk):
            c = (nchunk - 1 - step) if rev else step
            rows = slice(c * CHUNK, (c + 1) * CHUNK)
            for g, (lo, hi) in enumerate(_RW_GROUPS):
                ch = {name: val[rows, lo:hi] for name, val in full.items()}
                ch.update(pe=pe[c * CHUNK:c * CHUNK + 1, lo:hi], m=masks[(hi - lo, rev)],
                          d=d, g=g, step=step, rows=rows, cols=slice(lo, hi))
                chains.append(ch)

    qh, yi, mm, nn = _chunk_pieces(chains)

    y_refs = (yf_ref, yb_ref)
    h = [h_ref[...] for h_ref in h_refs]
    for step in range(nchunk):
        idx = [i for i, c in enumerate(chains) if c["step"] == step]
        hb = {i: h[chains[i]["d"] * ngroup + chains[i]["g"]].astype(BF16) for i in idx}
        ys = {i: _mm(qh[i], hb[i]) + yi[i] for i in idx}
        hn = {i: _mm(mm[i], hb[i]) + nn[i] for i in idx}
        for i in idx:
            c = chains[i]
            y_refs[c["d"]][c["rows"], c["cols"]] = ys[i]
            h[c["d"] * ngroup + c["g"]] = hn[i]
    for h_ref, hv in zip(h_refs, h):
        h_ref[...] = hv


def _bwd_tile(i, nct, nt):
    return jnp.where(i < nct, nct - 1 - i, nt - 1 - (i - nct))


def _rwkv(p_rw, w0, wup, a0, aup, k_k, k_a, nct):
    b, r, wcols = p_rw.shape
    nt = r // TM
    fwd = lambda bi, i: (bi, i, 0)
    bwd = lambda bi, i: (bi, _bwd_tile(i, nct, nt), 0)
    full = lambda shape: pl.BlockSpec(shape, lambda bi, i: (0,) * len(shape))
    scratch = [pltpu.VMEM((hi - lo, hi - lo), F32) for _ in range(2) for lo, hi in _RW_GROUPS]
    return pl.pallas_call(
        _rwkv_kernel,
        grid=(b, nt),
        in_specs=[
            pl.BlockSpec((None, TM, wcols), fwd),
            pl.BlockSpec((None, TM, wcols), bwd),
            full((2, RW_DIM)), full((2 * HEAD_DIM, RW_DIM)),
            full((2, RW_DIM)), full((2 * HEAD_DIM, RW_DIM)),
            full((1, RW_DIM)), full((1, RW_DIM)),
        ],
        out_specs=[pl.BlockSpec((None, TM, RW_DIM), fwd), pl.BlockSpec((None, TM, RW_DIM), bwd)],
        out_shape=[jax.ShapeDtypeStruct((b, r, RW_DIM), F32)] * 2,
        scratch_shapes=scratch,
        compiler_params=pltpu.CompilerParams(
            dimension_semantics=("parallel", "arbitrary"), vmem_limit_bytes=VMEM_LIMIT),
        name="rwkv",
    )(p_rw, p_rw, w0, wup, a0, aup, k_k, k_a)


def _attn_kernel(sink_ref, q_ref, kp_ref, kc_ref, kn_ref, kx_ref, o_ref, *, ncb, nb):
    j = pl.program_id(1)
    latent = j >= ncb
    has_prev = (j - 1) >= ncb
    has_next = (j + 1) <= nb - 1
    nloc = 3 * QB
    nkeys = nloc + kx_ref.shape[0]

    lane = lax.broadcasted_iota(jnp.int32, (QB, LANE), 1)
    low = lane < HEAD_DIM
    qt = [q_ref[:, LANE * t:LANE * (t + 1)] for t in range(AT_DIM // LANE)]

    row = lax.broadcasted_iota(jnp.int32, (3 * QB, nkeys), 0)
    col = lax.broadcasted_iota(jnp.int32, (3 * QB, nkeys), 1)
    tq = row % QB
    lo_b = jnp.maximum(tq, jnp.where(has_prev, 0, QB))
    hi_b = jnp.minimum(tq + 2 * WINDOW, jnp.where(has_next, nloc - 1, 2 * QB - 1))
    hi_b = jnp.where(latent, hi_b, -1)
    valid = ((col >= lo_b) & (col <= hi_b)) | (col >= nloc)
    hrow = lax.broadcasted_iota(jnp.int32, (3 * QB, 1), 0) // QB

    outs = []
    for g in range(2):
        if g == 0:
            qs = [jnp.where(low, qt[0], 0.0), jnp.where(low, 0.0, qt[0]), jnp.where(low, qt[1], 0.0)]
        else:
            qs = [jnp.where(low, 0.0, qt[1]), jnp.where(low, qt[2], 0.0), jnp.where(low, 0.0, qt[2])]
        qs = jnp.concatenate(qs, axis=0)
        kcols = slice(LANE * g, LANE * (g + 1))
        vcols = slice(2 * LANE + LANE * g, 2 * LANE + LANE * (g + 1))
        kall = jnp.concatenate([kp_ref[:, kcols], kc_ref[:, kcols], kn_ref[:, kcols], kx_ref[:, kcols]], axis=0)
        vall = jnp.concatenate([kp_ref[:, vcols], kc_ref[:, vcols], kn_ref[:, vcols], kx_ref[:, vcols]], axis=0)
        s = jnp.where(valid, _dot_nt(qs, kall), MASK_VALUE)
        sk = jnp.where(hrow == 0, sink_ref[3 * g], jnp.where(hrow == 1, sink_ref[3 * g + 1], sink_ref[3 * g + 2]))
        mx = jnp.maximum(jnp.max(s, axis=-1, keepdims=True), sk)
        p = jnp.exp(s - mx)
        den = jnp.sum(p, axis=-1, keepdims=True) + jnp.exp(sk - mx)
        outs.append(_dot(p, vall) / den)

    o0, o1 = outs
    o_ref[...] = jnp.concatenate([
        jnp.where(low, o0[0:QB], o0[QB:2 * QB]),
        jnp.where(low, o0[2 * QB:3 * QB], o1[0:QB]),
        jnp.where(low, o1[QB:2 * QB], o1[2 * QB:3 * QB]),
    ], axis=1)


def _attn(p_q, p_kv, sink, ctx_len):
    b, r, _ = p_q.shape
    nb = r // QB
    ncb = ctx_len // QB
    clamp = lambda j: jnp.clip(j, ncb, nb - 1)
    kvw = p_kv.shape[-1]
    return pl.pallas_call(
        functools.partial(_attn_kernel, ncb=ncb, nb=nb),
        grid=(b, nb),
        in_specs=[
            pl.BlockSpec(memory_space=pltpu.SMEM),
            pl.BlockSpec((None, QB, AT_DIM), lambda bi, j: (bi, j, 0)),
            pl.BlockSpec((None, QB, kvw), lambda bi, j: (bi, clamp(j - 1), 0)),
            pl.BlockSpec((None, QB, kvw), lambda bi, j: (bi, clamp(j), 0)),
            pl.BlockSpec((None, QB, kvw), lambda bi, j: (bi, clamp(j + 1), 0)),
            pl.BlockSpec((None, ctx_len, kvw), lambda bi, j: (bi, 0, 0)),
        ],
        out_specs=pl.BlockSpec((None, QB, AT_DIM), lambda bi, j: (bi, j, 0)),
        out_shape=jax.ShapeDtypeStruct((b, r, AT_DIM), F32),
        compiler_params=pltpu.CompilerParams(
            dimension_semantics=("parallel", "parallel"), vmem_limit_bytes=VMEM_LIMIT),
        name="attn",
    )(sink, p_q, p_kv, p_kv, p_kv, p_kv)


def _out_kernel(x_ref, mod_ref, rw_ref, yf_ref, yb_ref, at_ref, cv_ref, cvp_ref, cvn_ref,
                rk_ref, lng_ref, lnb_ref, cvw_ref, wout_ref, fg_ref, o_ref, *, tile_off, ctx_len, rows, final):
    i = pl.program_id(1) + tile_off
    row0 = i * TM
    r = rw_ref[:, 0:384]
    k = rw_ref[:, 384:768]
    v = rw_ref[:, 768:1152]
    g = rw_ref[:, 1152:1536]

    mean_h = _head_ones(RW_DIM, 1.0 / HEAD_DIM)
    y = yf_ref[...] + yb_ref[...]
    dev = y - _dot_split(y, mean_h)
    var = _dot_split(dev * dev, mean_h)
    yn = dev * lax.rsqrt(var + LNX_EPS) * lng_ref[...] + lnb_ref[...]
    bonus = _dot_split(r * k * rk_ref[...], _head_ones(RW_DIM)) * v
    y_rw = (yn + bonus) * (g * _sigmoid(g))

    g_at = cv_ref[:, 0:384]
    y_at = at_ref[...] * (g_at * _sigmoid(g_at))

    bg = cv_ref[:, 384:640]
    u = cv_ref[:, 640:896] * cv_ref[:, 896:1152]
    g_cv = cv_ref[:, 1152:1408]
    prev_ok = jnp.logical_and(row0 != 0, row0 != ctx_len)
    next_ok = jnp.logical_and(row0 + TM != ctx_len, row0 + TM != rows)
    u_prev = jnp.where(prev_ok, cvp_ref[7:8, 640:896] * cvp_ref[7:8, 896:1152], 0.0)
    u_next = jnp.where(next_ok, cvn_ref[0:1, 640:896] * cvn_ref[0:1, 896:1152], 0.0)
    ridx = lax.broadcasted_iota(jnp.int32, (TM, CV_DIM), 0)
    um = jnp.where(ridx == 0, u_prev, pltpu.roll(u, 1, 0))
    up = jnp.where(ridx == TM - 1, u_next, pltpu.roll(u, TM - 1, 0))
    conv = um * cvw_ref[0:1, :] + u * cvw_ref[1:2, :] + up * cvw_ref[2:3, :]
    y_cv = bg * conv * (g_cv * _sigmoid(g_cv))

    acc = (jnp.dot(y_rw.astype(BF16), wout_ref[0:384, :], preferred_element_type=F32)
           + jnp.dot(y_at.astype(BF16), wout_ref[384:768, :], preferred_element_type=F32)
           + jnp.dot(y_cv.astype(BF16), wout_ref[768:1024, :], preferred_element_type=F32))
    gate = mod_ref[:, 2 * D_MODEL:3 * D_MODEL]
    xn = x_ref[...] + gate * acc
    if final:
        ms = jnp.mean(xn * xn, axis=-1, keepdims=True)
        xn = xn * lax.rsqrt(ms + NORM_EPS) * fg_ref[...]
    o_ref[...] = xn


def _out(xa, mod, p_rw, y_f, y_b, y_at, p_cv, r_k, ln_g, ln_b, cv_w, w_out, final_g, nct, ctx_len, final):
    b, r, _ = xa.shape
    nt = r // TM
    off = nct if final else 0
    cvw = p_cv.shape[-1]
    hb = TM // 8
    row = lambda bi, i: (bi, i + off, 0)
    full = lambda shape: pl.BlockSpec(shape, lambda bi, i: (0,) * len(shape))
    out_rows = r - off * TM
    return pl.pallas_call(
        functools.partial(_out_kernel, tile_off=off, ctx_len=ctx_len, rows=r, final=final),
        grid=(b, nt - off),
        in_specs=[
            pl.BlockSpec((None, TM, D_MODEL), row),
            pl.BlockSpec((None, 1, 3 * D_MODEL), lambda bi, i: (jnp.where(i + off < nct, b, bi), 0, 0)),
            pl.BlockSpec((None, TM, 1536), row),
            pl.BlockSpec((None, TM, RW_DIM), row),
            pl.BlockSpec((None, TM, RW_DIM), row),
            pl.BlockSpec((None, TM, AT_DIM), row),
            pl.BlockSpec((None, TM, cvw), row),
            pl.BlockSpec((None, 8, cvw), lambda bi, i: (bi, jnp.maximum((i + off) * hb - 1, 0), 0)),
            pl.BlockSpec((None, 8, cvw), lambda bi, i: (bi, jnp.minimum((i + off + 1) * hb, r // 8 - 1), 0)),
            full((1, RW_DIM)), full((1, RW_DIM)), full((1, RW_DIM)),
            full((3, CV_DIM)), full((D_MODEL, D_MODEL)), full((1, D_MODEL)),
        ],
        out_specs=pl.BlockSpec((None, TM, D_MODEL), lambda bi, i: (bi, i, 0)),
        out_shape=jax.ShapeDtypeStruct((b, out_rows, D_MODEL), F32),
        compiler_params=pltpu.CompilerParams(
            dimension_semantics=("parallel", "parallel"), vmem_limit_bytes=VMEM_LIMIT),
        name="out",
    )(xa, mod, p_rw, y_f, y_b, y_at, p_cv, p_cv, p_cv, r_k, ln_g, ln_b, cv_w, w_out, final_g)


def _rope_tables(seq, ctx_len):
    t = jnp.arange(seq)
    nfreq = HEAD_DIM // 4
    inv = ROPE_THETA ** (-jnp.arange(nfreq, dtype=F32) / nfreq)
    ang = jnp.concatenate([(t // GRID_W).astype(F32)[:, None] * inv,
                           (t % GRID_W).astype(F32)[:, None] * inv], axis=-1)
    cos, sin = jnp.cos(ang), jnp.sin(ang)
    reps = LANE // HEAD_DIM
    cos_t = jnp.tile(jnp.concatenate([cos, cos], axis=-1), (1, reps))
    sin_t = jnp.tile(jnp.concatenate([-sin, sin], axis=-1), (1, reps))
    cos_t = jnp.concatenate([jnp.ones((ctx_len, LANE), F32), cos_t], axis=0)
    sin_t = jnp.concatenate([jnp.zeros((ctx_len, LANE), F32), sin_t], axis=0)
    return cos_t, sin_t


def kernel(x, c, ctx, c_ctx, w_ada, b_ada, norm_g, w_in, w_out, rw_w0, rw_w_up, rw_a0, rw_a_up,
           rw_k_k, rw_k_a, rw_r_k, rw_ln_g, rw_ln_b, at_sink, cv_w, final_g):
    b, seq, _ = x.shape
    ctx_len = ctx.shape[1]
    depth = w_in.shape[0]
    assert ctx_len % TM == 0 and seq % TM == 0 and b + 1 <= 8
    nct = ctx_len // TM

    xa = jnp.concatenate([ctx, x], axis=1)
    cvec = jnp.zeros((8, D_MODEL), F32).at[0:b].set(c).at[b].set(c_ctx)
    mod = _ada(cvec, w_ada, b_ada)
    cos_t, sin_t = _rope_tables(seq, ctx_len)
    w_in_b = w_in.astype(BF16)
    w_out_b = w_out.astype(BF16)

    for l in range(depth):
        mod_l = mod[l].reshape(8, 1, 3 * D_MODEL)
        p_rw, p_q, p_kv, p_cv = _proj(xa, mod_l, norm_g[l].reshape(1, -1), w_in_b[l], cos_t, sin_t, nct)
        y_f, y_b = _rwkv(p_rw, rw_w0[l], rw_w_up[l].reshape(2 * HEAD_DIM, RW_DIM), rw_a0[l],
                         rw_a_up[l].reshape(2 * HEAD_DIM, RW_DIM), rw_k_k[l].reshape(1, -1),
                         rw_k_a[l].reshape(1, -1), nct)
        y_at = _attn(p_q, p_kv, at_sink[l], ctx_len)
        xa = _out(xa, mod_l, p_rw, y_f, y_b, y_at, p_cv, rw_r_k[l].reshape(1, -1),
                  rw_ln_g[l].reshape(1, -1), rw_ln_b[l].reshape(1, -1), cv_w[l], w_out_b[l],
                  final_g.reshape(1, -1), nct, ctx_len, final=(l == depth - 1))
    return xa
```

```python
import functools

import jax
import jax.numpy as jnp
from jax import lax
from jax.experimental import pallas as pl
from jax.experimental.pallas import tpu as pltpu

F32 = jnp.float32
BF16 = jnp.bfloat16

D_MODEL = 1024
HEAD_DIM = 64
GRID_W = 64
ROPE_THETA = 10000.0
RW_DIM = 384
AT_DIM = 384
CV_DIM = 256
WINDOW = 128
NORM_EPS = 1e-6
LNX_EPS = 64e-5
MASK_VALUE = -1e30
DECAY_SCALE = 0.6065306597126334

COL_RW = (0, 1792)
COL_Q = (1792, 2176)
COL_KV = (2176, 2432)
COL_CV = (2432, 3840)
D_IN = 3840

TM = 256
PROJ_TILES = (768, 512, 256)
CHUNK = 64
QB = 128
LANE = 128
HALO = 16
GROUP_W = 256
VMEM_LIMIT = 56 * 1024 * 1024


def _mm(a, b):
    return jnp.dot(a, b, preferred_element_type=F32)


def _mm_nt(a, b):
    return lax.dot_general(a, b, (((1,), (1,)), ((), ())), preferred_element_type=F32)


def _dot_hi(a, b):
    return jnp.dot(a, b, preferred_element_type=F32, precision=lax.Precision.HIGHEST)


def _dot_split(a, b, passes=2, left_exact=False):
    acc = None
    rem = a if not left_exact else b
    for p in range(passes):
        hi = rem.astype(BF16)
        t = jnp.dot(a, hi, preferred_element_type=F32) if left_exact else jnp.dot(hi, b, preferred_element_type=F32)
        acc = t if acc is None else acc + t
        if p + 1 < passes:
            rem = rem - hi.astype(F32)
    return acc


def _sigmoid(x):
    return 1.0 / (1.0 + jnp.exp(-x))


def _head_ones(n, scale=1.0):
    r = lax.broadcasted_iota(jnp.int32, (n, n), 0) // HEAD_DIM
    c = lax.broadcasted_iota(jnp.int32, (n, n), 1) // HEAD_DIM
    return jnp.where(r == c, scale, 0.0).astype(BF16)


def _head_sum(x, scale=1.0, passes=2):
    parts = []
    for lo in range(0, x.shape[1], GROUP_W):
        hi = min(lo + GROUP_W, x.shape[1])
        parts.append(_dot_split(x[:, lo:hi], _head_ones(hi - lo, scale), passes))
    return jnp.concatenate(parts, axis=1)


def _ada_kernel(c_ref, w_ref, b_ref, o_ref):
    c = c_ref[...]
    o_ref[...] = _dot_hi(c * _sigmoid(c), w_ref[...]) + b_ref[...]


def _ada(cvec, w_ada, b_ada):
    depth = w_ada.shape[0]
    tn = 1024
    return pl.pallas_call(
        _ada_kernel,
        grid=(depth, 3 * D_MODEL // tn),
        in_specs=[
            pl.BlockSpec((8, D_MODEL), lambda l, n: (0, 0)),
            pl.BlockSpec((None, D_MODEL, tn), lambda l, n: (l, 0, n)),
            pl.BlockSpec((None, 1, tn), lambda l, n: (l, 0, n)),
        ],
        out_specs=pl.BlockSpec((None, 8, tn), lambda l, n: (l, 0, n)),
        out_shape=jax.ShapeDtypeStruct((depth, 8, 3 * D_MODEL), F32),
        compiler_params=pltpu.CompilerParams(vmem_limit_bytes=VMEM_LIMIT),
        name="ada",
    )(cvec, w_ada, b_ada.reshape(depth, 1, 3 * D_MODEL))


def _proj_kernel(x_ref, modb_ref, modc_ref, g_ref, w_ref, cos_ref, sin_ref, rw_ref, q_ref, kv_ref, cv_ref,
                 *, ctx_len):
    tm = x_ref.shape[0]
    x = x_ref[...]
    ms = jnp.mean(x * x, axis=-1, keepdims=True)
    y = x * lax.rsqrt(ms + NORM_EPS) * g_ref[...]
    row = pl.program_id(1) * tm + lax.broadcasted_iota(jnp.int32, (tm, 1), 0)
    is_ctx = row < ctx_len
    mb = modb_ref[...]
    mc = modc_ref[...]
    shift = jnp.where(is_ctx, mc[:, 0:D_MODEL], mb[:, 0:D_MODEL])
    scale = jnp.where(is_ctx, mc[:, D_MODEL:2 * D_MODEL], mb[:, D_MODEL:2 * D_MODEL])
    h = (y * (1.0 + scale) + shift).astype(BF16)

    def seg(cols):
        return jnp.dot(h, w_ref[:, cols[0]:cols[1]], preferred_element_type=F32)

    rw_ref[...] = seg(COL_RW).astype(BF16)
    cv_ref[...] = seg(COL_CV).astype(BF16)

    cos = cos_ref[...]
    sin = sin_ref[...]
    lane = lax.broadcasted_iota(jnp.int32, (tm, LANE), 1)
    first_half = (lane % HEAD_DIM) < (HEAD_DIM // 2)
    low_head = lane < HEAD_DIM

    def rope(t):
        rot = jnp.where(first_half, pltpu.roll(t, LANE - 32, 1), pltpu.roll(t, 32, 1))
        return t * cos + rot * sin

    def dup(t):
        sw = pltpu.roll(t, HEAD_DIM, 1)
        return jnp.where(low_head, t, sw), jnp.where(low_head, sw, t)

    q = seg(COL_Q)
    q_ref[...] = jnp.concatenate(
        [rope(q[:, LANE * j:LANE * (j + 1)]) * (HEAD_DIM ** -0.5) for j in range(AT_DIM // LANE)],
        axis=1).astype(BF16)
    kv = seg(COL_KV)
    k0, k1 = dup(rope(kv[:, 0:LANE]))
    v0, v1 = dup(kv[:, LANE:2 * LANE])
    kv_ref[...] = jnp.concatenate([k0, k1, v0, v1], axis=1).astype(BF16)


def _proj(xa, mod, norm_g, w_in, cos_t, sin_t, ctx_len):
    b, r, _ = xa.shape
    tm = next(t for t in PROJ_TILES if r % t == 0)
    row = lambda bi, i: (bi, i, 0)
    widths = (COL_RW[1] - COL_RW[0], AT_DIM, 4 * LANE, COL_CV[1] - COL_CV[0])
    return pl.pallas_call(
        functools.partial(_proj_kernel, ctx_len=ctx_len),
        grid=(b, r // tm),
        in_specs=[
            pl.BlockSpec((None, tm, D_MODEL), row),
            pl.BlockSpec((None, 1, 3 * D_MODEL), lambda bi, i: (bi, 0, 0)),
            pl.BlockSpec((None, 1, 3 * D_MODEL), lambda bi, i: (b, 0, 0)),
            pl.BlockSpec((1, D_MODEL), lambda bi, i: (0, 0)),
            pl.BlockSpec((D_MODEL, D_IN), lambda bi, i: (0, 0)),
            pl.BlockSpec((tm, LANE), lambda bi, i: (i, 0)),
            pl.BlockSpec((tm, LANE), lambda bi, i: (i, 0)),
        ],
        out_specs=[pl.BlockSpec((None, tm, w), row) for w in widths],
        out_shape=[jax.ShapeDtypeStruct((b, r, w), BF16) for w in widths],
        compiler_params=pltpu.CompilerParams(
            dimension_semantics=("parallel", "parallel"), vmem_limit_bytes=VMEM_LIMIT),
        name="proj",
    )(xa, mod, mod, norm_g, w_in, cos_t, sin_t)


_RW_FIELDS = ("at", "rt", "kt", "bt", "kh", "bh", "v")


def _group_masks(rev_lo, rev_hi):
    w = GROUP_W
    row = lax.broadcasted_iota(jnp.int32, (CHUNK, w), 0)
    lane = lax.broadcasted_iota(jnp.int32, (CHUNK, w), 1)
    s = lane % CHUNK
    x = row ^ s
    if rev_lo == rev_hi:
        before = (row - s) if rev_lo else (s - row)
    else:
        sign = jnp.where(lane >= LANE, -1 if rev_hi else 1, -1 if rev_lo else 1)
        before = (s - row) * sign
    m = {
        "strict": before < 0,
        "incl": before <= 0,
        "eye": s == row,
        "lvl1": (x >> 1) == 0,
        "heads": [(lane // HEAD_DIM) == h for h in range(w // HEAD_DIM)],
    }
    for lg in range(1, 6):
        m["lvl%d" % (1 << lg)] = (x >> lg) == 1
    return m


def _bd(x, m):
    return jnp.concatenate([jnp.where(hm, x, 0.0) for hm in m["heads"]], axis=0)


def _head_transpose(x):
    xt = x.T
    return jnp.concatenate([xt[h * HEAD_DIM:(h + 1) * HEAD_DIM] for h in range(x.shape[1] // HEAD_DIM)], axis=1)


def _chunk_pieces(ch):
    n = range(len(ch))
    ms = [c["m"] for c in ch]
    ar = [jnp.concatenate([c["at"], c["rt"]], axis=0) for c in ch]
    gb = [_mm(ar[i], _bd(_head_transpose(ch[i]["bt"]), ms[i])) for i in n]
    gk = [_mm(ar[i], _bd(_head_transpose(ch[i]["kt"]), ms[i])) for i in n]
    lo = [jnp.where(ms[i]["strict"], gb[i][:CHUNK], 0.0) for i in n]
    aak = [jnp.where(ms[i]["strict"], gk[i][:CHUNK], 0.0) for i in n]
    arb = [jnp.where(ms[i]["incl"], gb[i][CHUNK:], 0.0) for i in n]
    ark = [jnp.where(ms[i]["incl"], gk[i][CHUNK:], 0.0) for i in n]

    t = [jnp.where(ms[i]["eye"], 1.0, 0.0) - jnp.where(ms[i]["lvl1"], lo[i], 0.0) for i in n]
    for lvl in (2, 4, 8, 16, 32):
        x = [_mm(t[i], _bd(jnp.where(ms[i]["lvl%d" % lvl], lo[i], 0.0), ms[i])) for i in n]
        t = [t[i] - _mm(x[i], _bd(t[i], ms[i])) for i in n]

    c2 = 2 * CHUNK
    bht = [_head_transpose(c["bh"]) for c in ch]
    kht = [_head_transpose(c["kh"]) for c in ch]
    av = [_mm(jnp.concatenate([aak[i], ark[i], kht[i]], axis=0), _bd(ch[i]["v"], ms[i])) for i in n]
    z = [_mm(jnp.concatenate([arb[i], bht[i]], axis=0), _bd(t[i], ms[i])) for i in n]
    pw = [_mm(z[i], _bd(ch[i]["at"], ms[i])) for i in n]
    pu = [_mm(z[i], _bd(av[i][:CHUNK], ms[i])) for i in n]
    qh = [ch[i]["rt"] - pw[i][:CHUNK] for i in n]
    yi = [av[i][CHUNK:c2] - pu[i][:CHUNK] for i in n]
    mm = [jnp.where(ms[i]["eye"], ch[i]["pe"], 0.0) - pw[i][CHUNK:] for i in n]
    nn = [av[i][c2:] - pu[i][CHUNK:] for i in n]
    return qh, yi, mm, nn


def _rwkv_kernel(xf_ref, xb_ref, w0_ref, wup_ref, a0_ref, aup_ref, kk_ref, ka_ref,
                 yf_ref, yb_ref, *h_refs):
    nchunk = TM // CHUNK
    ngroup = len(h_refs)

    @pl.when(pl.program_id(1) == 0)
    def _():
        for h_ref in h_refs:
            h_ref[...] = jnp.zeros_like(h_ref)

    rt_i = lax.broadcasted_iota(jnp.int32, (TM, TM), 0)
    ct_i = lax.broadcasted_iota(jnp.int32, (TM, TM), 1)
    same_chunk = (rt_i // CHUNK) == (ct_i // CHUNK)
    lane = lax.broadcasted_iota(jnp.int32, (TM, LANE), 1)
    masks = [_group_masks(False, False), _group_masks(False, True), _group_masks(True, True)]

    full, pe = [], []
    for d, x_ref in enumerate((xf_ref, xb_ref)):
        rev = d == 1
        r = x_ref[:, 0:384].astype(F32)
        k = x_ref[:, 384:768].astype(F32)
        v = x_ref[:, 768:1152].astype(F32)
        dmask = (lane // HEAD_DIM) == d
        tw = jnp.where(dmask, jnp.tanh(x_ref[:, 1536:1664].astype(F32)), 0.0)
        ta = jnp.where(dmask, x_ref[:, 1664:1792].astype(F32), 0.0)
        lw = w0_ref[d:d + 1, :] + _mm(tw, wup_ref[...])
        logw = -DECAY_SCALE * _sigmoid(lw)
        a = _sigmoid(a0_ref[d:d + 1, :] + _mm(ta, aup_ref[...]))
        kk = k * kk_ref[...]
        kk = kk * lax.rsqrt(jnp.maximum(_head_sum(kk * kk), 1e-12))
        kd = k * (1.0 + (a - 1.0) * ka_ref[...])
        beta = kk * a
        tri = jnp.where(same_chunk & ((ct_i >= rt_i) if rev else (ct_i <= rt_i)), 1.0, 0.0).astype(BF16)
        cl = _dot_split(tri, logw, left_exact=True)
        ends = [(c * CHUNK) if rev else (c * CHUNK + CHUNK - 1) for c in range(nchunk)]
        pend = jnp.concatenate([jnp.broadcast_to(cl[e:e + 1, :], (CHUNK, RW_DIM)) for e in ends], axis=0)
        inv = jnp.exp(-cl)
        tail = jnp.exp(pend - cl)
        full.append(dict(at=kk * jnp.exp(cl - logw), rt=r * jnp.exp(cl), kt=kd * inv, bt=beta * inv,
                         kh=kd * tail, bh=beta * tail, v=v))
        pe.append(jnp.exp(pend))

    chains = []
    for step in range(nchunk):
        rows = (slice(step * CHUNK, (step + 1) * CHUNK),
                slice((nchunk - 1 - step) * CHUNK, (nchunk - step) * CHUNK))
        layout = (((0, 0, 256),), ((0, 256, 384), (1, 0, 128)), ((1, 128, 384),))
        for g, parts in enumerate(layout):
            def gather(arrs):
                cols = [arrs[d][rows[d], lo:hi] for d, lo, hi in parts]
                return cols[0] if len(cols) == 1 else jnp.concatenate(cols, axis=1)
            ch = {name: gather([full[0][name], full[1][name]]) for name in _RW_FIELDS}
            pe_g = gather(pe)
            ch.update(pe=pe_g[0:1, :], m=masks[g], g=g, step=step,
                      dst=[(d, rows[d], lo, hi) for d, lo, hi in parts])
            chains.append(ch)

    qh, yi, mm, nn = _chunk_pieces(chains)

    y_refs = (yf_ref, yb_ref)
    h = [h_ref[...] for h_ref in h_refs]
    for step in range(nchunk):
        idx = [i for i, c in enumerate(chains) if c["step"] == step]
        res = {i: _mm(jnp.concatenate([qh[i], mm[i]], axis=0), _bd(h[chains[i]["g"]], chains[i]["m"]))
               for i in idx}
        for i in idx:
            ys = res[i][:CHUNK] + yi[i]
            off = 0
            for d, rws, lo, hi in chains[i]["dst"]:
                y_refs[d][rws, lo:hi] = ys[:, off:off + hi - lo].astype(yf_ref.dtype)
                off += hi - lo
            h[chains[i]["g"]] = res[i][CHUNK:] + nn[i]
    for h_ref, hv in zip(h_refs, h):
        h_ref[...] = hv


def _bwd_tile(i, nct, nt):
    return jnp.where(i < nct, nct - 1 - i, nt - 1 - (i - nct))


def _rwkv(p_rw, w0, wup, a0, aup, k_k, k_a, nct):
    b, r, wcols = p_rw.shape
    nt = r // TM
    fwd = lambda bi, i: (bi, i, 0)
    bwd = lambda bi, i: (bi, _bwd_tile(i, nct, nt), 0)
    full = lambda shape: pl.BlockSpec(shape, lambda bi, i: (0,) * len(shape))
    scratch = [pltpu.VMEM((HEAD_DIM, GROUP_W), F32) for _ in range(2 * RW_DIM // GROUP_W)]
    return pl.pallas_call(
        _rwkv_kernel,
        grid=(b, nt),
        in_specs=[
            pl.BlockSpec((None, TM, wcols), fwd),
            pl.BlockSpec((None, TM, wcols), bwd),
            full((2, RW_DIM)), full((2 * HEAD_DIM, RW_DIM)),
            full((2, RW_DIM)), full((2 * HEAD_DIM, RW_DIM)),
            full((1, RW_DIM)), full((1, RW_DIM)),
        ],
        out_specs=[pl.BlockSpec((None, TM, RW_DIM), fwd), pl.BlockSpec((None, TM, RW_DIM), bwd)],
        out_shape=[jax.ShapeDtypeStruct((b, r, RW_DIM), BF16)] * 2,
        scratch_shapes=scratch,
        compiler_params=pltpu.CompilerParams(
            dimension_semantics=("parallel", "arbitrary"), vmem_limit_bytes=VMEM_LIMIT),
        name="rwkv",
    )(p_rw, p_rw, w0, wup, a0, aup, k_k, k_a)


def _attn_kernel(sink_ref, q_ref, kp_ref, kc_ref, kn_ref, kx_ref, o_ref, *, nct, nt):
    i = pl.program_id(1)
    latent = i >= nct
    nloc = 3 * QB
    nkeys = nloc + kx_ref.shape[0]

    lane = lax.broadcasted_iota(jnp.int32, (QB, LANE), 1)
    low = lane < HEAD_DIM
    low_m = jnp.where(low, 1.0, 0.0).astype(BF16)
    high_m = jnp.where(low, 0.0, 1.0).astype(BF16)
    row = lax.broadcasted_iota(jnp.int32, (3 * QB, nkeys), 0)
    col = lax.broadcasted_iota(jnp.int32, (3 * QB, nkeys), 1)
    tq = row % QB
    hrow = lax.broadcasted_iota(jnp.int32, (3 * QB, 1), 0) // QB
    head_src = (((0, low_m), (0, high_m), (1, low_m)), ((1, high_m), (2, low_m), (2, high_m)))

    qs, kall, vall, valid, sk = [], [], [], [], []
    for qb in range(TM // QB):
        qrows = slice(qb * QB, (qb + 1) * QB)
        qt = [q_ref[qrows, LANE * t:LANE * (t + 1)] for t in range(AT_DIM // LANE)]
        halves = [(kc_ref, slice(0, QB)), (kc_ref, slice(QB, 2 * QB))]
        if qb == 0:
            blocks = [(kp_ref, slice(0, QB))] + halves
            first_ok, last_ok = i > nct, True
        else:
            blocks = halves + [(kn_ref, slice(0, QB))]
            first_ok, last_ok = True, i < nt - 1
        lo_b = jnp.maximum(tq, jnp.where(first_ok, 0, QB))
        hi_b = jnp.minimum(tq + 2 * WINDOW, jnp.where(last_ok, nloc - 1, 2 * QB - 1))
        hi_b = jnp.where(latent, hi_b, -1)
        ok = ((col >= lo_b) & (col <= hi_b)) | (col >= nloc)
        for g in range(2):
            qs.append(jnp.concatenate([qt[t] * hm for t, hm in head_src[g]], axis=0))
            kcols = slice(LANE * g, LANE * (g + 1))
            vcols = slice(2 * LANE + LANE * g, 2 * LANE + LANE * (g + 1))
            kall.append(jnp.concatenate([ref[rs, kcols] for ref, rs in blocks] + [kx_ref[:, kcols]], axis=0))
            vall.append(jnp.concatenate([ref[rs, vcols] for ref, rs in blocks] + [kx_ref[:, vcols]], axis=0))
            valid.append(ok)
            sk.append(jnp.where(hrow == 0, sink_ref[3 * g],
                                jnp.where(hrow == 1, sink_ref[3 * g + 1], sink_ref[3 * g + 2])))

    n = range(len(qs))
    s = [jnp.where(valid[c], _mm_nt(qs[c], kall[c]), MASK_VALUE) for c in n]
    mx = [jnp.maximum(jnp.max(s[c], axis=-1, keepdims=True), sk[c]) for c in n]
    p = [jnp.exp(s[c] - mx[c]) for c in n]
    den = [jnp.sum(p[c], axis=-1, keepdims=True) + jnp.exp(sk[c] - mx[c]) for c in n]
    o = [_mm(p[c].astype(BF16), vall[c]) / den[c] for c in n]

    for qb in range(TM // QB):
        o0, o1 = o[2 * qb], o[2 * qb + 1]
        o_ref[qb * QB:(qb + 1) * QB, :] = jnp.concatenate([
            jnp.where(low, o0[0:QB], o0[QB:2 * QB]),
            jnp.where(low, o0[2 * QB:3 * QB], o1[0:QB]),
            jnp.where(low, o1[QB:2 * QB], o1[2 * QB:3 * QB]),
        ], axis=1).astype(o_ref.dtype)


def _attn(p_q, p_kv, sink, ctx_len):
    b, r, _ = p_q.shape
    nt = r // TM
    nct = ctx_len // TM
    per = TM // QB
    nb = r // QB
    clamp = lambda j: jnp.clip(j, nct * per, nb - 1)
    kvw = p_kv.shape[-1]
    return pl.pallas_call(
        functools.partial(_attn_kernel, nct=nct, nt=nt),
        grid=(b, nt),
        in_specs=[
            pl.BlockSpec(memory_space=pltpu.SMEM),
            pl.BlockSpec((None, TM, AT_DIM), lambda bi, i: (bi, i, 0)),
            pl.BlockSpec((None, QB, kvw), lambda bi, i: (bi, clamp(i * per - 1), 0)),
            pl.BlockSpec((None, TM, kvw), lambda bi, i: (bi, i, 0)),
            pl.BlockSpec((None, QB, kvw), lambda bi, i: (bi, clamp((i + 1) * per), 0)),
            pl.BlockSpec((None, ctx_len, kvw), lambda bi, i: (bi, 0, 0)),
        ],
        out_specs=pl.BlockSpec((None, TM, AT_DIM), lambda bi, i: (bi, i, 0)),
        out_shape=jax.ShapeDtypeStruct((b, r, AT_DIM), BF16),
        compiler_params=pltpu.CompilerParams(
            dimension_semantics=("parallel", "parallel"), vmem_limit_bytes=VMEM_LIMIT),
        name="attn",
    )(sink, p_q, p_kv, p_kv, p_kv, p_kv)


def _out_kernel(x_ref, mod_ref, rw_ref, yf_ref, yb_ref, at_ref, cv_ref, cvp_ref, cvn_ref,
                rk_ref, lng_ref, lnb_ref, cvw_ref, wout_ref, fg_ref, o_ref, *, tile_off, ctx_len, rows, final):
    i = pl.program_id(1) + tile_off
    row0 = i * TM
    r = rw_ref[:, 0:384].astype(F32)
    k = rw_ref[:, 384:768].astype(F32)
    v = rw_ref[:, 768:1152].astype(F32)
    g = rw_ref[:, 1152:1536].astype(F32)

    y = yf_ref[...].astype(F32) + yb_ref[...].astype(F32)
    dev = y - _head_sum(y, 1.0 / HEAD_DIM)
    var = _head_sum(dev * dev, 1.0 / HEAD_DIM, passes=1)
    yn = dev * lax.rsqrt(var + LNX_EPS) * lng_ref[...] + lnb_ref[...]
    bonus = _head_sum(r * k * rk_ref[...], passes=1) * v
    y_rw = (yn + bonus) * (g * _sigmoid(g))

    g_at = cv_ref[:, 0:384].astype(F32)
    y_at = at_ref[...].astype(F32) * (g_at * _sigmoid(g_at))

    bg = cv_ref[:, 384:640].astype(F32)
    u = cv_ref[:, 640:896].astype(F32) * cv_ref[:, 896:1152].astype(F32)
    g_cv = cv_ref[:, 1152:1408].astype(F32)
    prev_ok = jnp.logical_and(row0 != 0, row0 != ctx_len)
    next_ok = jnp.logical_and(row0 + TM != ctx_len, row0 + TM != rows)
    pu = cvp_ref[:, 640:896].astype(F32) * cvp_ref[:, 896:1152].astype(F32)
    nu = cvn_ref[:, 640:896].astype(F32) * cvn_ref[:, 896:1152].astype(F32)
    u_prev = jnp.where(prev_ok, pu[HALO - 1:HALO, :], 0.0)
    u_next = jnp.where(next_ok, nu[0:1, :], 0.0)
    ridx = lax.broadcasted_iota(jnp.int32, (TM, CV_DIM), 0)
    um = jnp.where(ridx == 0, u_prev, pltpu.roll(u, 1, 0))
    up = jnp.where(ridx == TM - 1, u_next, pltpu.roll(u, TM - 1, 0))
    conv = um * cvw_ref[0:1, :] + u * cvw_ref[1:2, :] + up * cvw_ref[2:3, :]
    y_cv = bg * conv * (g_cv * _sigmoid(g_cv))

    acc = (jnp.dot(y_rw.astype(BF16), wout_ref[0:384, :], preferred_element_type=F32)
           + jnp.dot(y_at.astype(BF16), wout_ref[384:768, :], preferred_element_type=F32)
           + jnp.dot(y_cv.astype(BF16), wout_ref[768:1024, :], preferred_element_type=F32))
    gate = mod_ref[:, 2 * D_MODEL:3 * D_MODEL]
    xn = x_ref[...] + gate * acc
    if final:
        ms = jnp.mean(xn * xn, axis=-1, keepdims=True)
        xn = xn * lax.rsqrt(ms + NORM_EPS) * fg_ref[...]
    o_ref[...] = xn


def _out(xa, mod, p_rw, y_f, y_b, y_at, p_cv, r_k, ln_g, ln_b, cv_w, w_out, final_g, nct, ctx_len, final):
    b, r, _ = xa.shape
    nt = r // TM
    off = nct if final else 0
    cvw = p_cv.shape[-1]
    hb = TM // HALO
    row = lambda bi, i: (bi, i + off, 0)
    full = lambda shape: pl.BlockSpec(shape, lambda bi, i: (0,) * len(shape))
    out_rows = r - off * TM
    return pl.pallas_call(
        functools.partial(_out_kernel, tile_off=off, ctx_len=ctx_len, rows=r, final=final),
        grid=(b, nt - off),
        in_specs=[
            pl.BlockSpec((None, TM, D_MODEL), row),
            pl.BlockSpec((None, 1, 3 * D_MODEL), lambda bi, i: (jnp.where(i + off < nct, b, bi), 0, 0)),
            pl.BlockSpec((None, TM, 1536), row),
            pl.BlockSpec((None, TM, RW_DIM), row),
            pl.BlockSpec((None, TM, RW_DIM), row),
            pl.BlockSpec((None, TM, AT_DIM), row),
            pl.BlockSpec((None, TM, cvw), row),
            pl.BlockSpec((None, HALO, cvw), lambda bi, i: (bi, jnp.maximum((i + off) * hb - 1, 0), 0)),
            pl.BlockSpec((None, HALO, cvw), lambda bi, i: (bi, jnp.minimum((i + off + 1) * hb, r // HALO - 1), 0)),
            full((1, RW_DIM)), full((1, RW_DIM)), full((1, RW_DIM)),
            full((3, CV_DIM)), full((D_MODEL, D_MODEL)), full((1, D_MODEL)),
        ],
        out_specs=pl.BlockSpec((None, TM, D_MODEL), lambda bi, i: (bi, i, 0)),
        out_shape=jax.ShapeDtypeStruct((b, out_rows, D_MODEL), F32),
        compiler_params=pltpu.CompilerParams(
            dimension_semantics=("parallel", "parallel"), vmem_limit_bytes=VMEM_LIMIT),
        name="out",
    )(xa, mod, p_rw, y_f, y_b, y_at, p_cv, p_cv, p_cv, r_k, ln_g, ln_b, cv_w, w_out, final_g)


def _rope_tables(seq, ctx_len):
    t = jnp.arange(seq)
    nfreq = HEAD_DIM // 4
    inv = ROPE_THETA ** (-jnp.arange(nfreq, dtype=F32) / nfreq)
    ang = jnp.concatenate([(t // GRID_W).astype(F32)[:, None] * inv,
                           (t % GRID_W).astype(F32)[:, None] * inv], axis=-1)
    cos, sin = jnp.cos(ang), jnp.sin(ang)
    reps = LANE // HEAD_DIM
    cos_t = jnp.tile(jnp.concatenate([cos, cos], axis=-1), (1, reps))
    sin_t = jnp.tile(jnp.concatenate([-sin, sin], axis=-1), (1, reps))
    cos_t = jnp.concatenate([jnp.ones((ctx_len, LANE), F32), cos_t], axis=0)
    sin_t = jnp.concatenate([jnp.zeros((ctx_len, LANE), F32), sin_t], axis=0)
    return cos_t, sin_t


def kernel(x, c, ctx, c_ctx, w_ada, b_ada, norm_g, w_in, w_out, rw_w0, rw_w_up, rw_a0, rw_a_up,
           rw_k_k, rw_k_a, rw_r_k, rw_ln_g, rw_ln_b, at_sink, cv_w, final_g):
    b, seq, _ = x.shape
    ctx_len = ctx.shape[1]
    depth = w_in.shape[0]
    assert ctx_len % TM == 0 and seq % TM == 0 and b + 1 <= 8
    nct = ctx_len // TM

    xa = jnp.concatenate([ctx, x], axis=1)
    cvec = jnp.zeros((8, D_MODEL), F32).at[0:b].set(c).at[b].set(c_ctx)
    mod = _ada(cvec, w_ada, b_ada)
    cos_t, sin_t = _rope_tables(seq, ctx_len)
    w_in_b = w_in.astype(BF16)
    w_out_b = w_out.astype(BF16)

    for l in range(depth):
        mod_l = mod[l].reshape(8, 1, 3 * D_MODEL)
        p_rw, p_q, p_kv, p_cv = _proj(xa, mod_l, norm_g[l].reshape(1, -1), w_in_b[l], cos_t, sin_t, ctx_len)
        y_f, y_b = _rwkv(p_rw, rw_w0[l], rw_w_up[l].reshape(2 * HEAD_DIM, RW_DIM), rw_a0[l],
                         rw_a_up[l].reshape(2 * HEAD_DIM, RW_DIM), rw_k_k[l].reshape(1, -1),
                         rw_k_a[l].reshape(1, -1), nct)
        y_at = _attn(p_q, p_kv, at_sink[l], ctx_len)
        xa = _out(xa, mod_l, p_rw, y_f, y_b, y_at, p_cv, rw_r_k[l].reshape(1, -1),
                  rw_ln_g[l].reshape(1, -1), rw_ln_b[l].reshape(1, -1), cv_w[l], w_out_b[l],
                  final_g.reshape(1, -1), nct, ctx_len, final=(l == depth - 1))
    return xa
```

```python
import functools

import jax
import jax.numpy as jnp
from jax import lax
from jax.experimental import pallas as pl
from jax.experimental.pallas import tpu as pltpu

F32 = jnp.float32
BF16 = jnp.bfloat16

D_MODEL = 1024
HEAD_DIM = 64
GRID_W = 64
ROPE_THETA = 10000.0
RW_DIM = 384
AT_DIM = 384
CV_DIM = 256
WINDOW = 128
NORM_EPS = 1e-6
LNX_EPS = 64e-5
MASK_VALUE = -1e30
DECAY_SCALE = 0.6065306597126334

COL_RW = (0, 1792)
COL_Q = (1792, 2176)
COL_KV = (2176, 2432)
COL_CV = (2432, 3840)
D_IN = 3840

TM = 256
PROJ_TILES = (768, 512, 256)
CHUNK = 64
QB = 128
LANE = 128
HALO = 16
GROUP_W = 256
VMEM_LIMIT = 56 * 1024 * 1024


def _mm(a, b):
    return jnp.dot(a, b, preferred_element_type=F32)


def _mm_nt(a, b):
    return lax.dot_general(a, b, (((1,), (1,)), ((), ())), preferred_element_type=F32)


def _dot_hi(a, b):
    return jnp.dot(a, b, preferred_element_type=F32, precision=lax.Precision.HIGHEST)


def _dot_split(a, b, passes=2, left_exact=False):
    acc = None
    rem = a if not left_exact else b
    for p in range(passes):
        hi = rem.astype(BF16)
        t = jnp.dot(a, hi, preferred_element_type=F32) if left_exact else jnp.dot(hi, b, preferred_element_type=F32)
        acc = t if acc is None else acc + t
        if p + 1 < passes:
            rem = rem - hi.astype(F32)
    return acc


def _sigmoid(x):
    return 1.0 / (1.0 + jnp.exp(-x))


def _head_ones(n, scale=1.0):
    r = lax.broadcasted_iota(jnp.int32, (n, n), 0) // HEAD_DIM
    c = lax.broadcasted_iota(jnp.int32, (n, n), 1) // HEAD_DIM
    return jnp.where(r == c, scale, 0.0).astype(BF16)


def _head_sum(x, scale=1.0, passes=2):
    parts = []
    for lo in range(0, x.shape[1], GROUP_W):
        hi = min(lo + GROUP_W, x.shape[1])
        parts.append(_dot_split(x[:, lo:hi], _head_ones(hi - lo, scale), passes))
    return jnp.concatenate(parts, axis=1)


def _ada_kernel(c_ref, w_ref, b_ref, o_ref):
    c = c_ref[...]
    o_ref[...] = _dot_hi(c * _sigmoid(c), w_ref[...]) + b_ref[...]


def _ada(cvec, w_ada, b_ada):
    depth = w_ada.shape[0]
    tn = 1024
    return pl.pallas_call(
        _ada_kernel,
        grid=(depth, 3 * D_MODEL // tn),
        in_specs=[
            pl.BlockSpec((8, D_MODEL), lambda l, n: (0, 0)),
            pl.BlockSpec((None, D_MODEL, tn), lambda l, n: (l, 0, n)),
            pl.BlockSpec((None, 1, tn), lambda l, n: (l, 0, n)),
        ],
        out_specs=pl.BlockSpec((None, 8, tn), lambda l, n: (l, 0, n)),
        out_shape=jax.ShapeDtypeStruct((depth, 8, 3 * D_MODEL), F32),
        compiler_params=pltpu.CompilerParams(vmem_limit_bytes=VMEM_LIMIT),
        name="ada",
    )(cvec, w_ada, b_ada.reshape(depth, 1, 3 * D_MODEL))


def _proj_kernel(x_ref, modb_ref, modc_ref, g_ref, w_ref, cos_ref, sin_ref, rw_ref, q_ref, kv_ref, cv_ref,
                 *, ctx_len):
    tm = x_ref.shape[0]
    x = x_ref[...]
    ms = jnp.mean(x * x, axis=-1, keepdims=True)
    y = x * lax.rsqrt(ms + NORM_EPS) * g_ref[...]
    row = pl.program_id(1) * tm + lax.broadcasted_iota(jnp.int32, (tm, 1), 0)
    is_ctx = row < ctx_len
    mb = modb_ref[...]
    mc = modc_ref[...]
    shift = jnp.where(is_ctx, mc[:, 0:D_MODEL], mb[:, 0:D_MODEL])
    scale = jnp.where(is_ctx, mc[:, D_MODEL:2 * D_MODEL], mb[:, D_MODEL:2 * D_MODEL])
    h = (y * (1.0 + scale) + shift).astype(BF16)

    def seg(cols):
        return jnp.dot(h, w_ref[:, cols[0]:cols[1]], preferred_element_type=F32)

    rw_ref[...] = seg(COL_RW).astype(BF16)
    cv_ref[...] = seg(COL_CV).astype(BF16)

    cos = cos_ref[...]
    sin = sin_ref[...]
    lane = lax.broadcasted_iota(jnp.int32, (tm, LANE), 1)
    first_half = (lane % HEAD_DIM) < (HEAD_DIM // 2)
    low_head = lane < HEAD_DIM

    def rope(t):
        rot = jnp.where(first_half, pltpu.roll(t, LANE - 32, 1), pltpu.roll(t, 32, 1))
        return t * cos + rot * sin

    def dup(t):
        sw = pltpu.roll(t, HEAD_DIM, 1)
        return jnp.where(low_head, t, sw), jnp.where(low_head, sw, t)

    q = seg(COL_Q)
    q_ref[...] = jnp.concatenate(
        [rope(q[:, LANE * j:LANE * (j + 1)]) * (HEAD_DIM ** -0.5) for j in range(AT_DIM // LANE)],
        axis=1).astype(BF16)
    kv = seg(COL_KV)
    k0, k1 = dup(rope(kv[:, 0:LANE]))
    v0, v1 = dup(kv[:, LANE:2 * LANE])
    kv_ref[...] = jnp.concatenate([k0, k1, v0, v1], axis=1).astype(BF16)


def _proj(xa, mod, norm_g, w_in, layer, cos_t, sin_t, ctx_len):
    b, r, _ = xa.shape
    tm = next(t for t in PROJ_TILES if r % t == 0)
    row = lambda bi, i: (bi, i, 0)
    widths = (COL_RW[1] - COL_RW[0], AT_DIM, 4 * LANE, COL_CV[1] - COL_CV[0])
    return pl.pallas_call(
        functools.partial(_proj_kernel, ctx_len=ctx_len),
        grid=(b, r // tm),
        in_specs=[
            pl.BlockSpec((None, tm, D_MODEL), row),
            pl.BlockSpec((None, 1, 3 * D_MODEL), lambda bi, i: (bi, 0, 0)),
            pl.BlockSpec((None, 1, 3 * D_MODEL), lambda bi, i: (b, 0, 0)),
            pl.BlockSpec((1, D_MODEL), lambda bi, i: (0, 0)),
            pl.BlockSpec((None, D_MODEL, D_IN), lambda bi, i: (layer, 0, 0)),
            pl.BlockSpec((tm, LANE), lambda bi, i: (i, 0)),
            pl.BlockSpec((tm, LANE), lambda bi, i: (i, 0)),
        ],
        out_specs=[pl.BlockSpec((None, tm, w), row) for w in widths],
        out_shape=[jax.ShapeDtypeStruct((b, r, w), BF16) for w in widths],
        compiler_params=pltpu.CompilerParams(
            dimension_semantics=("parallel", "parallel"), vmem_limit_bytes=VMEM_LIMIT),
        name="proj",
    )(xa, mod, mod, norm_g, w_in, cos_t, sin_t)


_RW_FIELDS = ("at", "rt", "kt", "bt", "kh", "bh", "v")


def _group_masks(rev_lo, rev_hi):
    w = GROUP_W
    row = lax.broadcasted_iota(jnp.int32, (CHUNK, w), 0)
    lane = lax.broadcasted_iota(jnp.int32, (CHUNK, w), 1)
    s = lane % CHUNK
    x = row ^ s
    if rev_lo == rev_hi:
        before = (row - s) if rev_lo else (s - row)
    else:
        sign = jnp.where(lane >= LANE, -1 if rev_hi else 1, -1 if rev_lo else 1)
        before = (s - row) * sign
    m = {
        "strict": before < 0,
        "incl": before <= 0,
        "eye": s == row,
        "lvl1": (x >> 1) == 0,
        "heads": [(lane // HEAD_DIM) == h for h in range(w // HEAD_DIM)],
    }
    for lg in range(1, 6):
        m["lvl%d" % (1 << lg)] = (x >> lg) == 1
    return m


def _bd(x, m):
    return jnp.concatenate([jnp.where(hm, x, 0.0) for hm in m["heads"]], axis=0)


def _head_transpose(x):
    xt = x.T
    return jnp.concatenate([xt[h * HEAD_DIM:(h + 1) * HEAD_DIM] for h in range(x.shape[1] // HEAD_DIM)], axis=1)


def _chunk_pieces(ch, side=()):
    def advance():
        for gen in side:
            next(gen, None)

    n = range(len(ch))
    ms = [c["m"] for c in ch]
    advance()
    ar = [jnp.concatenate([c["at"], c["rt"]], axis=0) for c in ch]
    gb = [_mm(ar[i], _bd(_head_transpose(ch[i]["bt"]), ms[i])) for i in n]
    gk = [_mm(ar[i], _bd(_head_transpose(ch[i]["kt"]), ms[i])) for i in n]
    lo = [jnp.where(ms[i]["strict"], gb[i][:CHUNK], 0.0) for i in n]
    aak = [jnp.where(ms[i]["strict"], gk[i][:CHUNK], 0.0) for i in n]
    arb = [jnp.where(ms[i]["incl"], gb[i][CHUNK:], 0.0) for i in n]
    ark = [jnp.where(ms[i]["incl"], gk[i][CHUNK:], 0.0) for i in n]

    t = [jnp.where(ms[i]["eye"], 1.0, 0.0) - jnp.where(ms[i]["lvl1"], lo[i], 0.0) for i in n]
    for lvl in (2, 4, 8, 16, 32):
        x = [_mm(t[i], _bd(jnp.where(ms[i]["lvl%d" % lvl], lo[i], 0.0), ms[i])) for i in n]
        t = [t[i] - _mm(x[i], _bd(t[i], ms[i])) for i in n]
        advance()

    c2 = 2 * CHUNK
    bht = [_head_transpose(c["bh"]) for c in ch]
    kht = [_head_transpose(c["kh"]) for c in ch]
    av = [_mm(jnp.concatenate([aak[i], ark[i], kht[i]], axis=0), _bd(ch[i]["v"], ms[i])) for i in n]
    z = [_mm(jnp.concatenate([arb[i], bht[i]], axis=0), _bd(t[i], ms[i])) for i in n]
    pw = [_mm(z[i], _bd(ch[i]["at"], ms[i])) for i in n]
    pu = [_mm(z[i], _bd(av[i][:CHUNK], ms[i])) for i in n]
    qh = [ch[i]["rt"] - pw[i][:CHUNK] for i in n]
    yi = [av[i][CHUNK:c2] - pu[i][:CHUNK] for i in n]
    mm = [jnp.where(ms[i]["eye"], ch[i]["pe"], 0.0) - pw[i][CHUNK:] for i in n]
    nn = [av[i][c2:] - pu[i][CHUNK:] for i in n]
    return qh, yi, mm, nn


_RW_SLOT_FIELDS = _RW_FIELDS + ("pe",)
_RW_LAYOUT = (((0, 0, 256),), ((0, 256, 384), (1, 0, 128)), ((1, 128, 384),))


def _rwkv_prep(x_ref, d, w0_ref, wup_ref, a0_ref, aup_ref, kk_ref, ka_ref, dst):
    nchunk = TM // CHUNK
    rev = d == 1
    rt_i = lax.broadcasted_iota(jnp.int32, (TM, TM), 0)
    ct_i = lax.broadcasted_iota(jnp.int32, (TM, TM), 1)
    same_chunk = (rt_i // CHUNK) == (ct_i // CHUNK)
    lane = lax.broadcasted_iota(jnp.int32, (TM, LANE), 1)
    r = x_ref[:, 0:384].astype(F32)
    k = x_ref[:, 384:768].astype(F32)
    dmask = (lane // HEAD_DIM) == d
    tw = jnp.where(dmask, jnp.tanh(x_ref[:, 1536:1664].astype(F32)), 0.0)
    ta = jnp.where(dmask, x_ref[:, 1664:1792].astype(F32), 0.0)
    lw = w0_ref[d:d + 1, :] + _mm(tw, wup_ref[...])
    la = a0_ref[d:d + 1, :] + _mm(ta, aup_ref[...])
    kk = k * kk_ref[...]
    ss = _head_sum(kk * kk, passes=1)
    yield
    logw = -DECAY_SCALE * _sigmoid(lw)
    tri = jnp.where(same_chunk & ((ct_i >= rt_i) if rev else (ct_i <= rt_i)), 1.0, 0.0).astype(BF16)
    cl = _dot_split(tri, logw, left_exact=True)
    yield
    a = _sigmoid(la)
    kk = kk * lax.rsqrt(jnp.maximum(ss, 1e-12))
    kd = k * (1.0 + (a - 1.0) * ka_ref[...])
    beta = kk * a
    ends = [(c * CHUNK) if rev else (c * CHUNK + CHUNK - 1) for c in range(nchunk)]
    pe = jnp.concatenate([jnp.broadcast_to(jnp.exp(cl[e:e + 1, :]), (CHUNK, RW_DIM)) for e in ends], axis=0)
    inv = jnp.exp(-cl)
    kt = kd * inv
    bt = beta * inv
    dst["at"][...] = kk * jnp.exp(cl - logw)
    dst["rt"][...] = r * jnp.exp(cl)
    dst["kt"][...] = kt
    dst["bt"][...] = bt
    dst["kh"][...] = kt * pe
    dst["bh"][...] = bt * pe
    dst["v"][...] = x_ref[:, 768:1152].astype(F32)
    dst["pe"][...] = pe


def _rwkv_chains(src, masks, groups):
    nchunk = TM // CHUNK
    chains = []
    for step in range(nchunk):
        rows = (slice(step * CHUNK, (step + 1) * CHUNK),
                slice((nchunk - 1 - step) * CHUNK, (nchunk - step) * CHUNK))
        for g in groups:
            parts = _RW_LAYOUT[g]

            def gather(name):
                cols = [src[d][name][rows[d], lo:hi] for d, lo, hi in parts]
                return cols[0] if len(cols) == 1 else jnp.concatenate(cols, axis=1)
            ch = {name: gather(name) for name in _RW_FIELDS}
            ch.update(pe=gather("pe")[0:1, :], m=masks[g], g=g, step=step,
                      dst=[(d, rows[d], lo, hi) for d, lo, hi in parts])
            chains.append(ch)
    return chains


def _rwkv_state_pass(chains, pieces, yf_ref, yb_ref, h_refs):
    qh, yi, mm, nn = pieces
    y_refs = (yf_ref, yb_ref)
    h = [h_ref[...] for h_ref in h_refs]
    for step in range(TM // CHUNK):
        idx = [i for i, c in enumerate(chains) if c["step"] == step]
        res = {i: _mm(jnp.concatenate([qh[i], mm[i]], axis=0), _bd(h[chains[i]["g"]], chains[i]["m"]))
               for i in idx}
        for i in idx:
            ys = res[i][:CHUNK] + yi[i]
            off = 0
            for d, rws, lo, hi in chains[i]["dst"]:
                y_refs[d][rws, lo:hi] = ys[:, off:off + hi - lo].astype(yf_ref.dtype)
                off += hi - lo
            h[chains[i]["g"]] = res[i][CHUNK:] + nn[i]
    for h_ref, hv in zip(h_refs, h):
        h_ref[...] = hv


def _rwkv_kernel(xf_ref, xb_ref, w0_ref, wup_ref, a0_ref, aup_ref, kk_ref, ka_ref,
                 yf_ref, yb_ref, *scratch):
    ngroup = len(_RW_LAYOUT)
    h_refs = scratch[:ngroup]
    nf = len(_RW_SLOT_FIELDS)
    prepared = [dict(zip(_RW_SLOT_FIELDS, scratch[ngroup + d * nf:ngroup + (d + 1) * nf])) for d in range(2)]
    params = (w0_ref, wup_ref, a0_ref, aup_ref, kk_ref, ka_ref)

    @pl.when(pl.program_id(1) == 0)
    def _():
        for ref in h_refs:
            ref[...] = jnp.zeros_like(ref)

    masks = [_group_masks(False, False), _group_masks(False, True), _group_masks(True, True)]
    for _ in _rwkv_prep(xf_ref, 0, *params, prepared[0]):
        pass
    prep_b = _rwkv_prep(xb_ref, 1, *params, prepared[1])
    chains_a = _rwkv_chains(prepared, masks, (0,))
    pieces_a = _chunk_pieces(chains_a, (prep_b,))
    for _ in prep_b:
        pass
    chains_b = _rwkv_chains(prepared, masks, (1, 2))
    pieces_b = _chunk_pieces(chains_b)
    pieces = [a + b for a, b in zip(pieces_a, pieces_b)]
    _rwkv_state_pass(chains_a + chains_b, pieces, yf_ref, yb_ref, h_refs)


def _bwd_tile(i, nct, nt):
    return jnp.where(i < nct, nct - 1 - i, nt - 1 - (i - nct))


def _rwkv(p_rw, w0, wup, a0, aup, k_k, k_a, nct):
    b, r, wcols = p_rw.shape
    nt = r // TM
    fwd = lambda bi, j: (bi, j, 0)
    bwd = lambda bi, j: (bi, _bwd_tile(j, nct, nt), 0)
    full = lambda shape: pl.BlockSpec(shape, lambda bi, j: (0,) * len(shape))
    scratch = [pltpu.VMEM((HEAD_DIM, GROUP_W), F32) for _ in _RW_LAYOUT]
    scratch += [pltpu.VMEM((TM, RW_DIM), F32) for _ in range(2 * len(_RW_SLOT_FIELDS))]
    return pl.pallas_call(
        _rwkv_kernel,
        grid=(b, nt),
        in_specs=[
            pl.BlockSpec((None, TM, wcols), fwd),
            pl.BlockSpec((None, TM, wcols), bwd),
            full((2, RW_DIM)), full((2 * HEAD_DIM, RW_DIM)),
            full((2, RW_DIM)), full((2 * HEAD_DIM, RW_DIM)),
            full((1, RW_DIM)), full((1, RW_DIM)),
        ],
        out_specs=[pl.BlockSpec((None, TM, RW_DIM), fwd), pl.BlockSpec((None, TM, RW_DIM), bwd)],
        out_shape=[jax.ShapeDtypeStruct((b, r, RW_DIM), BF16)] * 2,
        scratch_shapes=scratch,
        compiler_params=pltpu.CompilerParams(
            dimension_semantics=("parallel", "arbitrary"), vmem_limit_bytes=VMEM_LIMIT),
        name="rwkv",
    )(p_rw, p_rw, w0, wup, a0, aup, k_k, k_a)


def _attn_kernel(sink_ref, q_ref, kp_ref, kc_ref, kn_ref, kx_ref, o_ref, *, nct, nt):
    i = pl.program_id(1)
    latent = i >= nct
    nloc = 3 * QB
    nkeys = nloc + kx_ref.shape[0]

    lane = lax.broadcasted_iota(jnp.int32, (QB, LANE), 1)
    low = lane < HEAD_DIM
    low_m = jnp.where(low, 1.0, 0.0).astype(BF16)
    high_m = jnp.where(low, 0.0, 1.0).astype(BF16)
    row = lax.broadcasted_iota(jnp.int32, (3 * QB, nkeys), 0)
    col = lax.broadcasted_iota(jnp.int32, (3 * QB, nkeys), 1)
    tq = row % QB
    hrow = lax.broadcasted_iota(jnp.int32, (3 * QB, 1), 0) // QB
    head_src = (((0, low_m), (0, high_m), (1, low_m)), ((1, high_m), (2, low_m), (2, high_m)))

    qs, kall, vall, valid, sk = [], [], [], [], []
    for qb in range(TM // QB):
        qrows = slice(qb * QB, (qb + 1) * QB)
        qt = [q_ref[qrows, LANE * t:LANE * (t + 1)] for t in range(AT_DIM // LANE)]
        halves = [(kc_ref, slice(0, QB)), (kc_ref, slice(QB, 2 * QB))]
        if qb == 0:
            blocks = [(kp_ref, slice(0, QB))] + halves
            first_ok, last_ok = i > nct, True
        else:
            blocks = halves + [(kn_ref, slice(0, QB))]
            first_ok, last_ok = True, i < nt - 1
        lo_b = jnp.maximum(tq, jnp.where(first_ok, 0, QB))
        hi_b = jnp.minimum(tq + 2 * WINDOW, jnp.where(last_ok, nloc - 1, 2 * QB - 1))
        hi_b = jnp.where(latent, hi_b, -1)
        ok = ((col >= lo_b) & (col <= hi_b)) | (col >= nloc)
        for g in range(2):
            qs.append(jnp.concatenate([qt[t] * hm for t, hm in head_src[g]], axis=0))
            kcols = slice(LANE * g, LANE * (g + 1))
            vcols = slice(2 * LANE + LANE * g, 2 * LANE + LANE * (g + 1))
            kall.append(jnp.concatenate([ref[rs, kcols] for ref, rs in blocks] + [kx_ref[:, kcols]], axis=0))
            vall.append(jnp.concatenate([ref[rs, vcols] for ref, rs in blocks] + [kx_ref[:, vcols]], axis=0))
            valid.append(ok)
            sk.append(jnp.where(hrow == 0, sink_ref[3 * g],
                                jnp.where(hrow == 1, sink_ref[3 * g + 1], sink_ref[3 * g + 2])))

    n = range(len(qs))
    s = [jnp.where(valid[c], _mm_nt(qs[c], kall[c]), MASK_VALUE) for c in n]
    mx = [jnp.maximum(jnp.max(s[c], axis=-1, keepdims=True), sk[c]) for c in n]
    p = [jnp.exp(s[c] - mx[c]) for c in n]
    den = [jnp.sum(p[c], axis=-1, keepdims=True) + jnp.exp(sk[c] - mx[c]) for c in n]
    o = [_mm(p[c].astype(BF16), vall[c]) / den[c] for c in n]

    for qb in range(TM // QB):
        o0, o1 = o[2 * qb], o[2 * qb + 1]
        o_ref[qb * QB:(qb + 1) * QB, :] = jnp.concatenate([
            jnp.where(low, o0[0:QB], o0[QB:2 * QB]),
            jnp.where(low, o0[2 * QB:3 * QB], o1[0:QB]),
            jnp.where(low, o1[QB:2 * QB], o1[2 * QB:3 * QB]),
        ], axis=1).astype(o_ref.dtype)


def _attn(p_q, p_kv, sink, ctx_len):
    b, r, _ = p_q.shape
    nt = r // TM
    nct = ctx_len // TM
    per = TM // QB
    nb = r // QB
    clamp = lambda j: jnp.clip(j, nct * per, nb - 1)
    kvw = p_kv.shape[-1]
    return pl.pallas_call(
        functools.partial(_attn_kernel, nct=nct, nt=nt),
        grid=(b, nt),
        in_specs=[
            pl.BlockSpec(memory_space=pltpu.SMEM),
            pl.BlockSpec((None, TM, AT_DIM), lambda bi, i: (bi, i, 0)),
            pl.BlockSpec((None, QB, kvw), lambda bi, i: (bi, clamp(i * per - 1), 0)),
            pl.BlockSpec((None, TM, kvw), lambda bi, i: (bi, i, 0)),
            pl.BlockSpec((None, QB, kvw), lambda bi, i: (bi, clamp((i + 1) * per), 0)),
            pl.BlockSpec((None, ctx_len, kvw), lambda bi, i: (bi, 0, 0)),
        ],
        out_specs=pl.BlockSpec((None, TM, AT_DIM), lambda bi, i: (bi, i, 0)),
        out_shape=jax.ShapeDtypeStruct((b, r, AT_DIM), BF16),
        compiler_params=pltpu.CompilerParams(
            dimension_semantics=("parallel", "parallel"), vmem_limit_bytes=VMEM_LIMIT),
        name="attn",
    )(sink, p_q, p_kv, p_kv, p_kv, p_kv)


def _out_kernel(x_ref, mod_ref, rw_ref, yf_ref, yb_ref, at_ref, cv_ref, cvp_ref, cvn_ref,
                rk_ref, lng_ref, lnb_ref, cvw_ref, wout_ref, fg_ref, o_ref, *, tile_off, ctx_len, rows, final):
    i = pl.program_id(1) + tile_off
    row0 = i * TM
    r = rw_ref[:, 0:384].astype(F32)
    k = rw_ref[:, 384:768].astype(F32)
    v = rw_ref[:, 768:1152].astype(F32)
    g = rw_ref[:, 1152:1536].astype(F32)

    y = yf_ref[...].astype(F32) + yb_ref[...].astype(F32)
    dev = y - _head_sum(y, 1.0 / HEAD_DIM)
    var = _head_sum(dev * dev, 1.0 / HEAD_DIM, passes=1)
    yn = dev * lax.rsqrt(var + LNX_EPS) * lng_ref[...] + lnb_ref[...]
    bonus = _head_sum(r * k * rk_ref[...], passes=1) * v
    y_rw = (yn + bonus) * (g * _sigmoid(g))

    g_at = cv_ref[:, 0:384].astype(F32)
    y_at = at_ref[...].astype(F32) * (g_at * _sigmoid(g_at))

    bg = cv_ref[:, 384:640].astype(F32)
    u = cv_ref[:, 640:896].astype(F32) * cv_ref[:, 896:1152].astype(F32)
    g_cv = cv_ref[:, 1152:1408].astype(F32)
    prev_ok = jnp.logical_and(row0 != 0, row0 != ctx_len)
    next_ok = jnp.logical_and(row0 + TM != ctx_len, row0 + TM != rows)
    pu = cvp_ref[:, 640:896].astype(F32) * cvp_ref[:, 896:1152].astype(F32)
    nu = cvn_ref[:, 640:896].astype(F32) * cvn_ref[:, 896:1152].astype(F32)
    u_prev = jnp.where(prev_ok, pu[HALO - 1:HALO, :], 0.0)
    u_next = jnp.where(next_ok, nu[0:1, :], 0.0)
    ridx = lax.broadcasted_iota(jnp.int32, (TM, CV_DIM), 0)
    um = jnp.where(ridx == 0, u_prev, pltpu.roll(u, 1, 0))
    up = jnp.where(ridx == TM - 1, u_next, pltpu.roll(u, TM - 1, 0))
    conv = um * cvw_ref[0:1, :] + u * cvw_ref[1:2, :] + up * cvw_ref[2:3, :]
    y_cv = bg * conv * (g_cv * _sigmoid(g_cv))

    acc = (jnp.dot(y_rw.astype(BF16), wout_ref[0:384, :], preferred_element_type=F32)
           + jnp.dot(y_at.astype(BF16), wout_ref[384:768, :], preferred_element_type=F32)
           + jnp.dot(y_cv.astype(BF16), wout_ref[768:1024, :], preferred_element_type=F32))
    gate = mod_ref[:, 2 * D_MODEL:3 * D_MODEL]
    xn = x_ref[...] + gate * acc
    if final:
        ms = jnp.mean(xn * xn, axis=-1, keepdims=True)
        xn = xn * lax.rsqrt(ms + NORM_EPS) * fg_ref[...]
    o_ref[...] = xn


def _out(xa, mod, p_rw, y_f, y_b, y_at, p_cv, r_k, ln_g, ln_b, cv_w, w_out, layer, final_g, nct, ctx_len,
         final):
    b, r, _ = xa.shape
    nt = r // TM
    off = nct if final else 0
    cvw = p_cv.shape[-1]
    hb = TM // HALO
    row = lambda bi, i: (bi, i + off, 0)
    full = lambda shape: pl.BlockSpec(shape, lambda bi, i: (0,) * len(shape))
    out_rows = r - off * TM
    return pl.pallas_call(
        functools.partial(_out_kernel, tile_off=off, ctx_len=ctx_len, rows=r, final=final),
        grid=(b, nt - off),
        in_specs=[
            pl.BlockSpec((None, TM, D_MODEL), row),
            pl.BlockSpec((None, 1, 3 * D_MODEL), lambda bi, i: (jnp.where(i + off < nct, b, bi), 0, 0)),
            pl.BlockSpec((None, TM, 1536), row),
            pl.BlockSpec((None, TM, RW_DIM), row),
            pl.BlockSpec((None, TM, RW_DIM), row),
            pl.BlockSpec((None, TM, AT_DIM), row),
            pl.BlockSpec((None, TM, cvw), row),
            pl.BlockSpec((None, HALO, cvw), lambda bi, i: (bi, jnp.maximum((i + off) * hb - 1, 0), 0)),
            pl.BlockSpec((None, HALO, cvw), lambda bi, i: (bi, jnp.minimum((i + off + 1) * hb, r // HALO - 1), 0)),
            full((1, RW_DIM)), full((1, RW_DIM)), full((1, RW_DIM)),
            full((3, CV_DIM)),
            pl.BlockSpec((None, D_MODEL, D_MODEL), lambda bi, i: (layer, 0, 0)),
            full((1, D_MODEL)),
        ],
        out_specs=pl.BlockSpec((None, TM, D_MODEL), lambda bi, i: (bi, i, 0)),
        out_shape=jax.ShapeDtypeStruct((b, out_rows, D_MODEL), F32),
        compiler_params=pltpu.CompilerParams(
            dimension_semantics=("parallel", "parallel"), vmem_limit_bytes=VMEM_LIMIT),
        name="out",
    )(xa, mod, p_rw, y_f, y_b, y_at, p_cv, p_cv, p_cv, r_k, ln_g, ln_b, cv_w, w_out, final_g)


def _rope_tables(seq, ctx_len):
    t = jnp.arange(seq)
    nfreq = HEAD_DIM // 4
    inv = ROPE_THETA ** (-jnp.arange(nfreq, dtype=F32) / nfreq)
    ang = jnp.concatenate([(t // GRID_W).astype(F32)[:, None] * inv,
                           (t % GRID_W).astype(F32)[:, None] * inv], axis=-1)
    cos, sin = jnp.cos(ang), jnp.sin(ang)
    reps = LANE // HEAD_DIM
    cos_t = jnp.tile(jnp.concatenate([cos, cos], axis=-1), (1, reps))
    sin_t = jnp.tile(jnp.concatenate([-sin, sin], axis=-1), (1, reps))
    cos_t = jnp.concatenate([jnp.ones((ctx_len, LANE), F32), cos_t], axis=0)
    sin_t = jnp.concatenate([jnp.zeros((ctx_len, LANE), F32), sin_t], axis=0)
    return cos_t, sin_t


def kernel(x, c, ctx, c_ctx, w_ada, b_ada, norm_g, w_in, w_out, rw_w0, rw_w_up, rw_a0, rw_a_up,
           rw_k_k, rw_k_a, rw_r_k, rw_ln_g, rw_ln_b, at_sink, cv_w, final_g):
    b, seq, _ = x.shape
    ctx_len = ctx.shape[1]
    depth = w_in.shape[0]
    assert ctx_len % TM == 0 and seq % TM == 0 and b + 1 <= 8
    nct = ctx_len // TM

    xa = jnp.concatenate([ctx, x], axis=1)
    cvec = jnp.zeros((8, D_MODEL), F32).at[0:b].set(c).at[b].set(c_ctx)
    mod = _ada(cvec, w_ada, b_ada)
    cos_t, sin_t = _rope_tables(seq, ctx_len)
    w_in_b = w_in.astype(BF16)
    w_out_b = w_out.astype(BF16)

    for l in range(depth):
        mod_l = mod[l].reshape(8, 1, 3 * D_MODEL)
        p_rw, p_q, p_kv, p_cv = _proj(xa, mod_l, norm_g[l].reshape(1, -1), w_in_b, l, cos_t, sin_t, ctx_len)
        y_f, y_b = _rwkv(p_rw, rw_w0[l], rw_w_up[l].reshape(2 * HEAD_DIM, RW_DIM), rw_a0[l],
                         rw_a_up[l].reshape(2 * HEAD_DIM, RW_DIM), rw_k_k[l].reshape(1, -1),
                         rw_k_a[l].reshape(1, -1), nct)
        y_at = _attn(p_q, p_kv, at_sink[l], ctx_len)
        xa = _out(xa, mod_l, p_rw, y_f, y_b, y_at, p_cv, rw_r_k[l].reshape(1, -1),
                  rw_ln_g[l].reshape(1, -1), rw_ln_b[l].reshape(1, -1), cv_w[l], w_out_b, l,
                  final_g.reshape(1, -1), nct, ctx_len, final=(l == depth - 1))
    return xa
```

```python
import functools

import jax
import jax.numpy as jnp
from jax import lax
from jax.experimental import pallas as pl
from jax.experimental.pallas import tpu as pltpu

F32 = jnp.float32
BF16 = jnp.bfloat16

D_MODEL = 1024
HEAD_DIM = 64
GRID_W = 64
ROPE_THETA = 10000.0
RW_DIM = 384
AT_DIM = 384
CV_DIM = 256
WINDOW = 128
NORM_EPS = 1e-6
LNX_EPS = 64e-5
MASK_VALUE = -1e30
DECAY_SCALE = 0.6065306597126334
LOG2E = 1.4426950408889634
Q_SCALE = HEAD_DIM ** -0.5 * LOG2E

COL_RW = (0, 1792)
COL_Q = (1792, 2176)
COL_KV = (2176, 2432)
COL_CV = (2432, 3840)
D_IN = 3840

TM = 256
PROJ_TILES = (768, 512, 256)
CHUNK = 64
QB = 128
LANE = 128
HALO = 16
GROUP_W = 256
VMEM_LIMIT = 56 * 1024 * 1024


def _mm(a, b):
    return jnp.dot(a, b, preferred_element_type=F32)


def _mm_nt(a, b):
    return lax.dot_general(a, b, (((1,), (1,)), ((), ())), preferred_element_type=F32)


def _dot_hi(a, b):
    return jnp.dot(a, b, preferred_element_type=F32, precision=lax.Precision.HIGHEST)


def _dot_split(a, b, passes=2, left_exact=False):
    acc = None
    rem = a if not left_exact else b
    for p in range(passes):
        hi = rem.astype(BF16)
        t = jnp.dot(a, hi, preferred_element_type=F32) if left_exact else jnp.dot(hi, b, preferred_element_type=F32)
        acc = t if acc is None else acc + t
        if p + 1 < passes:
            rem = rem - hi.astype(F32)
    return acc


def _sigmoid(x):
    return 1.0 / (1.0 + jnp.exp(-x))


def _head_ones(n, scale=1.0):
    r = lax.broadcasted_iota(jnp.int32, (n, n), 0) // HEAD_DIM
    c = lax.broadcasted_iota(jnp.int32, (n, n), 1) // HEAD_DIM
    return jnp.where(r == c, scale, 0.0).astype(BF16)


def _head_sum(x, scale=1.0, passes=2):
    parts = []
    for lo in range(0, x.shape[1], GROUP_W):
        hi = min(lo + GROUP_W, x.shape[1])
        parts.append(_dot_split(x[:, lo:hi], _head_ones(hi - lo, scale), passes))
    return jnp.concatenate(parts, axis=1)


def _ada_kernel(c_ref, w_ref, b_ref, o_ref):
    c = c_ref[...]
    o_ref[...] = _dot_hi(c * _sigmoid(c), w_ref[...]) + b_ref[...]


def _ada(cvec, w_ada, b_ada):
    depth = w_ada.shape[0]
    tn = 1024
    return pl.pallas_call(
        _ada_kernel,
        grid=(depth, 3 * D_MODEL // tn),
        in_specs=[
            pl.BlockSpec((8, D_MODEL), lambda l, n: (0, 0)),
            pl.BlockSpec((None, D_MODEL, tn), lambda l, n: (l, 0, n)),
            pl.BlockSpec((None, 1, tn), lambda l, n: (l, 0, n)),
        ],
        out_specs=pl.BlockSpec((None, 8, tn), lambda l, n: (l, 0, n)),
        out_shape=jax.ShapeDtypeStruct((depth, 8, 3 * D_MODEL), F32),
        compiler_params=pltpu.CompilerParams(vmem_limit_bytes=VMEM_LIMIT),
        name="ada",
    )(cvec, w_ada, b_ada.reshape(depth, 1, 3 * D_MODEL))


def _stream_tile(x_refs, tile0, nct):
    if len(x_refs) == 1:
        return x_refs[0][...]
    nsub = len(x_refs) // 2
    blocks = [jnp.where(tile0 + k < nct, x_refs[nsub + k][...], x_refs[k][...]) for k in range(nsub)]
    return blocks[0] if nsub == 1 else jnp.concatenate(blocks, axis=0)


def _stream_specs(split, nsub, nct, off=0):
    if not split:
        return [pl.BlockSpec((None, nsub * TM, D_MODEL), lambda bi, i: (bi, i + off, 0))]
    lat = [pl.BlockSpec((None, TM, D_MODEL),
                        lambda bi, i, k=k: (bi, jnp.maximum((i + off) * nsub + k - nct, 0), 0)) for k in range(nsub)]
    ctx = [pl.BlockSpec((None, TM, D_MODEL),
                        lambda bi, i, k=k: (bi, jnp.minimum((i + off) * nsub + k, nct - 1), 0)) for k in range(nsub)]
    return lat + ctx


def _proj_kernel(*refs, ctx_len, nx):
    x_refs, (modb_ref, modc_ref, g_ref, w_ref, cos_ref, sin_ref, rw_ref, q_ref, kv_ref, cv_ref) = refs[:nx], refs[nx:]
    tm = rw_ref.shape[0]
    i = pl.program_id(1)
    x = _stream_tile(x_refs, i * (tm // TM), ctx_len // TM)
    _proj_compute(x, i * tm, modb_ref, modc_ref, g_ref, w_ref, cos_ref, sin_ref,
                  rw_ref, q_ref, kv_ref, cv_ref, ctx_len)


def _proj_compute(x, row0, modb_ref, modc_ref, g_ref, w_ref, cos_ref, sin_ref, rw_ref, q_ref, kv_ref, cv_ref,
                  ctx_len):
    tm = x.shape[0]
    ms = jnp.mean(x * x, axis=-1, keepdims=True)
    y = x * lax.rsqrt(ms + NORM_EPS) * g_ref[...]
    row = row0 + lax.broadcasted_iota(jnp.int32, (tm, 1), 0)
    is_ctx = row < ctx_len
    mb = modb_ref[...]
    mc = modc_ref[...]
    shift = jnp.where(is_ctx, mc[:, 0:D_MODEL], mb[:, 0:D_MODEL])
    scale = jnp.where(is_ctx, mc[:, D_MODEL:2 * D_MODEL], mb[:, D_MODEL:2 * D_MODEL])
    h = y * (1.0 + scale) + shift

    def seg(cols):
        return jnp.dot(h, w_ref[:, cols[0]:cols[1]], preferred_element_type=F32)

    rw_ref[...] = seg(COL_RW).astype(BF16)
    cv_ref[...] = seg(COL_CV).astype(BF16)

    cos = cos_ref[...]
    sin = sin_ref[...]
    lane = lax.broadcasted_iota(jnp.int32, (tm, LANE), 1)
    first_half = (lane % HEAD_DIM) < (HEAD_DIM // 2)
    low_head = lane < HEAD_DIM

    def rope(t):
        rot = jnp.where(first_half, pltpu.roll(t, LANE - 32, 1), pltpu.roll(t, 32, 1))
        return t * cos + rot * sin

    def dup(t):
        sw = pltpu.roll(t, HEAD_DIM, 1)
        return jnp.where(low_head, t, sw), jnp.where(low_head, sw, t)

    q = seg(COL_Q)
    q_ref[...] = jnp.concatenate(
        [rope(q[:, LANE * j:LANE * (j + 1)]) * Q_SCALE for j in range(AT_DIM // LANE)],
        axis=1).astype(BF16)
    kv = seg(COL_KV)
    k0, k1 = dup(rope(kv[:, 0:LANE]))
    v0, v1 = dup(kv[:, LANE:2 * LANE])
    kv_ref[...] = jnp.concatenate([k0, k1, v0, v1], axis=1).astype(BF16)


def _stream_operands(stream, nsub):
    if isinstance(stream, tuple):
        x, ctx = stream
        return [x] * nsub + [ctx] * nsub, True, x.shape[0], x.shape[1] + ctx.shape[1]
    return [stream], False, stream.shape[0], stream.shape[1]


def _proj(stream, mod, norm_g, w_in, layer, cos_t, sin_t, ctx_len):
    _, split, b, r = _stream_operands(stream, 1)
    tm = next(t for t in PROJ_TILES if r % t == 0)
    nsub = tm // TM
    x_ops = _stream_operands(stream, nsub)[0]
    x_specs = _stream_specs(split, nsub, ctx_len // TM)
    row = lambda bi, i: (bi, i, 0)
    widths = (COL_RW[1] - COL_RW[0], AT_DIM, 4 * LANE, COL_CV[1] - COL_CV[0])
    return pl.pallas_call(
        functools.partial(_proj_kernel, ctx_len=ctx_len, nx=len(x_specs)),
        grid=(b, r // tm),
        in_specs=x_specs + [
            pl.BlockSpec((None, 1, 3 * D_MODEL), lambda bi, i: (bi, 0, 0)),
            pl.BlockSpec((None, 1, 3 * D_MODEL), lambda bi, i: (b, 0, 0)),
            pl.BlockSpec((1, D_MODEL), lambda bi, i: (0, 0)),
            pl.BlockSpec((None, D_MODEL, D_IN), lambda bi, i: (layer, 0, 0), pipeline_mode=pl.Buffered(1)),
            pl.BlockSpec((tm, LANE), lambda bi, i: (i, 0)),
            pl.BlockSpec((tm, LANE), lambda bi, i: (i, 0)),
        ],
        out_specs=[pl.BlockSpec((None, tm, w), row) for w in widths],
        out_shape=[jax.ShapeDtypeStruct((b, r, w), BF16) for w in widths],
        compiler_params=pltpu.CompilerParams(
            dimension_semantics=("parallel", "parallel"), vmem_limit_bytes=VMEM_LIMIT),
        name="proj",
    )(*x_ops, mod, mod, norm_g, w_in, cos_t, sin_t)


_RW_FIELDS = ("at", "rt", "kt", "bt", "kh", "bh", "v")


def _group_masks(rev_lo, rev_hi):
    w = GROUP_W
    row = lax.broadcasted_iota(jnp.int32, (CHUNK, w), 0)
    lane = lax.broadcasted_iota(jnp.int32, (CHUNK, w), 1)
    s = lane % CHUNK
    x = row ^ s
    if rev_lo == rev_hi:
        before = (row - s) if rev_lo else (s - row)
    else:
        sign = jnp.where(lane >= LANE, -1 if rev_hi else 1, -1 if rev_lo else 1)
        before = (s - row) * sign
    m = {
        "strict": before < 0,
        "incl": before <= 0,
        "eye": s == row,
        "lvl1": (x >> 1) == 0,
        "heads": [(lane // HEAD_DIM) == h for h in range(w // HEAD_DIM)],
    }
    for lg in range(1, 6):
        m["lvl%d" % (1 << lg)] = (x >> lg) == 1
    return m


def _bd(x, m):
    return jnp.concatenate([jnp.where(hm, x, 0.0) for hm in m["heads"]], axis=0)


def _head_transpose(x):
    xt = x.T
    return jnp.concatenate([xt[h * HEAD_DIM:(h + 1) * HEAD_DIM] for h in range(x.shape[1] // HEAD_DIM)], axis=1)


def _chunk_pieces(ch, side=()):
    def advance():
        for gen in side:
            next(gen, None)

    n = range(len(ch))
    ms = [c["m"] for c in ch]
    advance()
    ar = [jnp.concatenate([c["at"], c["rt"]], axis=0) for c in ch]
    gb = [_mm(ar[i], _bd(_head_transpose(ch[i]["bt"]), ms[i])) for i in n]
    gk = [_mm(ar[i], _bd(_head_transpose(ch[i]["kt"]), ms[i])) for i in n]
    lo = [jnp.where(ms[i]["strict"], gb[i][:CHUNK], 0.0) for i in n]
    aak = [jnp.where(ms[i]["strict"], gk[i][:CHUNK], 0.0) for i in n]
    arb = [jnp.where(ms[i]["incl"], gb[i][CHUNK:], 0.0) for i in n]
    ark = [jnp.where(ms[i]["incl"], gk[i][CHUNK:], 0.0) for i in n]

    t = [jnp.where(ms[i]["eye"], 1.0, 0.0) - jnp.where(ms[i]["lvl1"], lo[i], 0.0) for i in n]
    for lvl in (2, 4, 8, 16, 32):
        x = [_mm(t[i], _bd(jnp.where(ms[i]["lvl%d" % lvl], lo[i], 0.0), ms[i])) for i in n]
        t = [t[i] - _mm(x[i], _bd(t[i], ms[i])) for i in n]
        advance()

    c2 = 2 * CHUNK
    bht = [_head_transpose(c["bh"]) for c in ch]
    kht = [_head_transpose(c["kh"]) for c in ch]
    av = [_mm(jnp.concatenate([aak[i], ark[i], kht[i]], axis=0), _bd(ch[i]["v"], ms[i])) for i in n]
    z = [_mm(jnp.concatenate([arb[i], bht[i]], axis=0), _bd(t[i], ms[i])) for i in n]
    pw = [_mm(z[i], _bd(ch[i]["at"], ms[i])) for i in n]
    pu = [_mm(z[i], _bd(av[i][:CHUNK], ms[i])) for i in n]
    qh = [ch[i]["rt"] - pw[i][:CHUNK] for i in n]
    yi = [av[i][CHUNK:c2] - pu[i][:CHUNK] for i in n]
    mm = [jnp.where(ms[i]["eye"], ch[i]["pe"], 0.0) - pw[i][CHUNK:] for i in n]
    nn = [av[i][c2:] - pu[i][CHUNK:] for i in n]
    return qh, yi, mm, nn


_RW_SLOT_FIELDS = _RW_FIELDS + ("pe",)
_RW_LAYOUT = (((0, 0, 256),), ((0, 256, 384), (1, 0, 128)), ((1, 128, 384),))


def _rwkv_prep(x_ref, d, w0_ref, wup_ref, a0_ref, aup_ref, kk_ref, ka_ref, dst):
    nchunk = TM // CHUNK
    rev = d == 1
    rt_i = lax.broadcasted_iota(jnp.int32, (TM, TM), 0)
    ct_i = lax.broadcasted_iota(jnp.int32, (TM, TM), 1)
    same_chunk = (rt_i // CHUNK) == (ct_i // CHUNK)
    lane = lax.broadcasted_iota(jnp.int32, (TM, LANE), 1)
    r = x_ref[:, 0:384].astype(F32)
    k = x_ref[:, 384:768].astype(F32)
    dmask = (lane // HEAD_DIM) == d
    tw = jnp.where(dmask, jnp.tanh(x_ref[:, 1536:1664].astype(F32)), 0.0)
    ta = jnp.where(dmask, x_ref[:, 1664:1792].astype(F32), 0.0)
    lw = w0_ref[d:d + 1, :] + _mm(tw, wup_ref[...])
    la = a0_ref[d:d + 1, :] + _mm(ta, aup_ref[...])
    kk = k * kk_ref[...]
    ss = _head_sum(kk * kk, passes=1)
    yield
    logw = -(DECAY_SCALE * LOG2E) * _sigmoid(lw)
    tri = jnp.where(same_chunk & ((ct_i >= rt_i) if rev else (ct_i <= rt_i)), 1.0, 0.0).astype(BF16)
    cl = _dot_split(tri, logw, left_exact=True)
    yield
    a = _sigmoid(la)
    kk = kk * lax.rsqrt(jnp.maximum(ss, 1e-12))
    kd = k * (1.0 + (a - 1.0) * ka_ref[...])
    beta = kk * a
    ends = [(c * CHUNK) if rev else (c * CHUNK + CHUNK - 1) for c in range(nchunk)]
    pe = jnp.concatenate([jnp.broadcast_to(jnp.exp2(cl[e:e + 1, :]), (CHUNK, RW_DIM)) for e in ends], axis=0)
    inv = jnp.exp2(-cl)
    kt = kd * inv
    bt = beta * inv
    dst["at"][...] = kk * jnp.exp2(cl - logw)
    dst["rt"][...] = r * jnp.exp2(cl)
    dst["kt"][...] = kt
    dst["bt"][...] = bt
    dst["kh"][...] = kt * pe
    dst["bh"][...] = bt * pe
    dst["v"][...] = x_ref[:, 768:1152].astype(F32)
    dst["pe"][...] = pe


def _rwkv_chains(src, masks, groups):
    nchunk = TM // CHUNK
    chains = []
    for step in range(nchunk):
        rows = (slice(step * CHUNK, (step + 1) * CHUNK),
                slice((nchunk - 1 - step) * CHUNK, (nchunk - step) * CHUNK))
        for g in groups:
            parts = _RW_LAYOUT[g]

            def gather(name):
                cols = [src[d][name][rows[d], lo:hi] for d, lo, hi in parts]
                return cols[0] if len(cols) == 1 else jnp.concatenate(cols, axis=1)
            ch = {name: gather(name) for name in _RW_FIELDS}
            ch.update(pe=gather("pe")[0:1, :], m=masks[g], g=g, step=step,
                      dst=[(d, rows[d], lo, hi) for d, lo, hi in parts])
            chains.append(ch)
    return chains


def _rwkv_state_pass(chains, pieces, yf_ref, yb_ref, h_refs):
    qh, yi, mm, nn = pieces
    y_refs = (yf_ref, yb_ref)
    h = [h_ref[...] for h_ref in h_refs]
    for step in range(TM // CHUNK):
        idx = [i for i, c in enumerate(chains) if c["step"] == step]
        res = {i: _mm(jnp.concatenate([qh[i], mm[i]], axis=0), _bd(h[chains[i]["g"]], chains[i]["m"]))
               for i in idx}
        for i in idx:
            ys = res[i][:CHUNK] + yi[i]
            off = 0
            for d, rws, lo, hi in chains[i]["dst"]:
                y_refs[d][rws, lo:hi] = ys[:, off:off + hi - lo].astype(yf_ref.dtype)
                off += hi - lo
            h[chains[i]["g"]] = res[i][CHUNK:] + nn[i]
    for h_ref, hv in zip(h_refs, h):
        h_ref[...] = hv


def _rwkv_kernel(xf_ref, xb_ref, w0_ref, wup_ref, a0_ref, aup_ref, kk_ref, ka_ref,
                 yf_ref, yb_ref, *scratch):
    ngroup = len(_RW_LAYOUT)
    h_refs = scratch[:ngroup]
    nf = len(_RW_SLOT_FIELDS)
    prepared = [dict(zip(_RW_SLOT_FIELDS, scratch[ngroup + d * nf:ngroup + (d + 1) * nf])) for d in range(2)]
    params = (w0_ref, wup_ref, a0_ref, aup_ref, kk_ref, ka_ref)

    @pl.when(pl.program_id(1) == 0)
    def _():
        for ref in h_refs:
            ref[...] = jnp.zeros_like(ref)

    masks = [_group_masks(False, False), _group_masks(False, True), _group_masks(True, True)]
    for _ in _rwkv_prep(xf_ref, 0, *params, prepared[0]):
        pass
    prep_b = _rwkv_prep(xb_ref, 1, *params, prepared[1])
    chains_a = _rwkv_chains(prepared, masks, (0,))
    pieces_a = _chunk_pieces(chains_a, (prep_b,))
    for _ in prep_b:
        pass
    chains_b = _rwkv_chains(prepared, masks, (1, 2))
    pieces_b = _chunk_pieces(chains_b)
    pieces = [a + b for a, b in zip(pieces_a, pieces_b)]
    _rwkv_state_pass(chains_a + chains_b, pieces, yf_ref, yb_ref, h_refs)


def _bwd_tile(i, nct, nt):
    return jnp.where(i < nct, nct - 1 - i, nt - 1 - (i - nct))


def _rwkv(p_rw, w0, wup, a0, aup, k_k, k_a, nct):
    b, r, wcols = p_rw.shape
    nt = r // TM
    fwd = lambda bi, j: (bi, j, 0)
    bwd = lambda bi, j: (bi, _bwd_tile(j, nct, nt), 0)
    full = lambda shape: pl.BlockSpec(shape, lambda bi, j: (0,) * len(shape))
    scratch = [pltpu.VMEM((HEAD_DIM, GROUP_W), F32) for _ in _RW_LAYOUT]
    scratch += [pltpu.VMEM((TM, RW_DIM), F32) for _ in range(2 * len(_RW_SLOT_FIELDS))]
    return pl.pallas_call(
        _rwkv_kernel,
        grid=(b, nt),
        in_specs=[
            pl.BlockSpec((None, TM, wcols), fwd),
            pl.BlockSpec((None, TM, wcols), bwd),
            full((2, RW_DIM)), full((2 * HEAD_DIM, RW_DIM)),
            full((2, RW_DIM)), full((2 * HEAD_DIM, RW_DIM)),
            full((1, RW_DIM)), full((1, RW_DIM)),
        ],
        out_specs=[pl.BlockSpec((None, TM, RW_DIM), fwd), pl.BlockSpec((None, TM, RW_DIM), bwd)],
        out_shape=[jax.ShapeDtypeStruct((b, r, RW_DIM), BF16)] * 2,
        scratch_shapes=scratch,
        compiler_params=pltpu.CompilerParams(
            dimension_semantics=("parallel", "arbitrary"), vmem_limit_bytes=VMEM_LIMIT),
        name="rwkv",
    )(p_rw, p_rw, w0, wup, a0, aup, k_k, k_a)


def _attn_kernel(sink_ref, band_ref, q_ref, kp_ref, kc_ref, kn_ref, kx_ref, o_ref, *, nct, nt):
    i = pl.program_id(1)
    latent = i >= nct
    nloc = 3 * QB
    nkeys = nloc + kx_ref.shape[0]

    lane = lax.broadcasted_iota(jnp.int32, (QB, LANE), 1)
    low = lane < HEAD_DIM
    low_m = jnp.where(low, 1.0, 0.0).astype(BF16)
    high_m = jnp.where(low, 0.0, 1.0).astype(BF16)
    col = lax.broadcasted_iota(jnp.int32, (1, nkeys), 1)
    hrow = lax.broadcasted_iota(jnp.int32, (3 * QB, 1), 0) // QB
    band = band_ref[...]
    head_src = (((0, low_m), (0, high_m), (1, low_m)), ((1, high_m), (2, low_m), (2, high_m)))

    qs, kall, vall, bias, sk = [], [], [], [], []
    for qb in range(TM // QB):
        qrows = slice(qb * QB, (qb + 1) * QB)
        qt = [q_ref[qrows, LANE * t:LANE * (t + 1)] for t in range(AT_DIM // LANE)]
        halves = [(kc_ref, slice(0, QB)), (kc_ref, slice(QB, 2 * QB))]
        if qb == 0:
            blocks = [(kp_ref, slice(0, QB))] + halves
            first_ok, last_ok = i > nct, True
        else:
            blocks = halves + [(kn_ref, slice(0, QB))]
            first_ok, last_ok = True, i < nt - 1
        lo_c = jnp.where(jnp.logical_and(latent, first_ok), 0, jnp.where(latent, QB, nloc))
        hi_c = jnp.where(last_ok, nloc, 2 * QB)
        dead = jnp.logical_and(col < nloc, jnp.logical_or(col < lo_c, col >= hi_c))
        bias_qb = band + jnp.where(dead, MASK_VALUE, 0.0)
        for g in range(2):
            qs.append(jnp.concatenate([qt[t] * hm for t, hm in head_src[g]], axis=0))
            kcols = slice(LANE * g, LANE * (g + 1))
            vcols = slice(2 * LANE + LANE * g, 2 * LANE + LANE * (g + 1))
            kall.append(jnp.concatenate([ref[rs, kcols] for ref, rs in blocks] + [kx_ref[:, kcols]], axis=0))
            vall.append(jnp.concatenate([ref[rs, vcols] for ref, rs in blocks] + [kx_ref[:, vcols]], axis=0))
            bias.append(bias_qb)
            sk.append(LOG2E * jnp.where(hrow == 0, sink_ref[3 * g],
                                        jnp.where(hrow == 1, sink_ref[3 * g + 1], sink_ref[3 * g + 2])))

    n = range(len(qs))
    s = [_mm_nt(qs[c], kall[c]) + bias[c] for c in n]
    mx = [jnp.maximum(jnp.max(s[c], axis=-1, keepdims=True), sk[c]) for c in n]
    p = [jnp.exp2(s[c] - mx[c]) for c in n]
    den = [jnp.sum(p[c], axis=-1, keepdims=True) + jnp.exp2(sk[c] - mx[c]) for c in n]
    o = [_mm(p[c].astype(BF16), vall[c]) / den[c] for c in n]

    for qb in range(TM // QB):
        o0, o1 = o[2 * qb], o[2 * qb + 1]
        o_ref[qb * QB:(qb + 1) * QB, :] = jnp.concatenate([
            jnp.where(low, o0[0:QB], o0[QB:2 * QB]),
            jnp.where(low, o0[2 * QB:3 * QB], o1[0:QB]),
            jnp.where(low, o1[QB:2 * QB], o1[2 * QB:3 * QB]),
        ], axis=1).astype(o_ref.dtype)


def _attn_band(ctx_len):
    t = jnp.arange(3 * QB)[:, None] % QB
    col = jnp.arange(3 * QB + ctx_len)[None, :]
    ok = ((col >= t) & (col <= t + 2 * WINDOW)) | (col >= 3 * QB)
    return jnp.where(ok, 0.0, MASK_VALUE).astype(F32)


def _attn(p_q, p_kv, sink, band, ctx_len):
    b, r, _ = p_q.shape
    nt = r // TM
    nct = ctx_len // TM
    per = TM // QB
    nb = r // QB
    clamp = lambda j: jnp.clip(j, nct * per, nb - 1)
    kvw = p_kv.shape[-1]
    return pl.pallas_call(
        functools.partial(_attn_kernel, nct=nct, nt=nt),
        grid=(b, nt),
        in_specs=[
            pl.BlockSpec(memory_space=pltpu.SMEM),
            pl.BlockSpec(band.shape, lambda bi, i: (0, 0)),
            pl.BlockSpec((None, TM, AT_DIM), lambda bi, i: (bi, i, 0)),
            pl.BlockSpec((None, QB, kvw), lambda bi, i: (bi, clamp(i * per - 1), 0)),
            pl.BlockSpec((None, TM, kvw), lambda bi, i: (bi, i, 0)),
            pl.BlockSpec((None, QB, kvw), lambda bi, i: (bi, clamp((i + 1) * per), 0)),
            pl.BlockSpec((None, ctx_len, kvw), lambda bi, i: (bi, 0, 0)),
        ],
        out_specs=pl.BlockSpec((None, TM, AT_DIM), lambda bi, i: (bi, i, 0)),
        out_shape=jax.ShapeDtypeStruct((b, r, AT_DIM), BF16),
        compiler_params=pltpu.CompilerParams(
            dimension_semantics=("parallel", "parallel"), vmem_limit_bytes=VMEM_LIMIT),
        name="attn",
    )(sink, band, p_q, p_kv, p_kv, p_kv, p_kv)


def _out_kernel(*refs, nx, tile_off, ctx_len, rows, final):
    x_refs, rest = refs[:nx], refs[nx:]
    fg_ref, o_ref = rest[-2:]
    tile = pl.program_id(1) + tile_off
    x = _stream_tile(x_refs, tile, ctx_len // TM)
    xn = _out_compute(tile * TM, x, *rest[:-2], ctx_len, rows)
    if final:
        ms = jnp.mean(xn * xn, axis=-1, keepdims=True)
        xn = xn * lax.rsqrt(ms + NORM_EPS) * fg_ref[...]
    o_ref[...] = xn


def _out_compute(row0, x, mod_ref, rw_ref, yf_ref, yb_ref, at_ref, cv_ref, cvp_ref, cvn_ref,
                 rk_ref, lng_ref, lnb_ref, cvw_ref, wout_ref, ctx_len, rows):
    r = rw_ref[:, 0:384].astype(F32)
    k = rw_ref[:, 384:768].astype(F32)
    v = rw_ref[:, 768:1152].astype(F32)
    g = rw_ref[:, 1152:1536].astype(F32)

    y = yf_ref[...].astype(F32) + yb_ref[...].astype(F32)
    dev = y - _head_sum(y, 1.0 / HEAD_DIM)
    var = _head_sum(dev * dev, 1.0 / HEAD_DIM, passes=1)
    yn = dev * lax.rsqrt(var + LNX_EPS) * lng_ref[...] + lnb_ref[...]
    bonus = _head_sum(r * k * rk_ref[...], passes=1) * v
    y_rw = (yn + bonus) * (g * _sigmoid(g))

    g_at = cv_ref[:, 0:384].astype(F32)
    y_at = at_ref[...].astype(F32) * (g_at * _sigmoid(g_at))

    bg = cv_ref[:, 384:640].astype(F32)
    u = cv_ref[:, 640:896].astype(F32) * cv_ref[:, 896:1152].astype(F32)
    g_cv = cv_ref[:, 1152:1408].astype(F32)
    prev_ok = jnp.logical_and(row0 != 0, row0 != ctx_len)
    next_ok = jnp.logical_and(row0 + TM != ctx_len, row0 + TM != rows)
    pu = cvp_ref[:, 640:896].astype(F32) * cvp_ref[:, 896:1152].astype(F32)
    nu = cvn_ref[:, 640:896].astype(F32) * cvn_ref[:, 896:1152].astype(F32)
    u_prev = jnp.where(prev_ok, pu[HALO - 1:HALO, :], 0.0)
    u_next = jnp.where(next_ok, nu[0:1, :], 0.0)
    ridx = lax.broadcasted_iota(jnp.int32, (TM, CV_DIM), 0)
    um = jnp.where(ridx == 0, u_prev, pltpu.roll(u, 1, 0))
    up = jnp.where(ridx == TM - 1, u_next, pltpu.roll(u, TM - 1, 0))
    conv = um * cvw_ref[0:1, :] + u * cvw_ref[1:2, :] + up * cvw_ref[2:3, :]
    y_cv = bg * conv * (g_cv * _sigmoid(g_cv))

    acc = _mm(y_rw, wout_ref[0:384, :]) + _mm(y_at, wout_ref[384:768, :]) + _mm(y_cv, wout_ref[768:1024, :])
    gate = mod_ref[:, 2 * D_MODEL:3 * D_MODEL]
    return x + gate * acc


def _out_in_specs(b, r, off, nct, cvw, layer):
    hb = TM // HALO
    row = lambda bi, i: (bi, i + off, 0)
    full = lambda shape: pl.BlockSpec(shape, lambda bi, i: (0,) * len(shape))
    return [
        pl.BlockSpec((None, 1, 3 * D_MODEL), lambda bi, i: (jnp.where(i + off < nct, b, bi), 0, 0)),
        pl.BlockSpec((None, TM, 1536), row),
        pl.BlockSpec((None, TM, RW_DIM), row),
        pl.BlockSpec((None, TM, RW_DIM), row),
        pl.BlockSpec((None, TM, AT_DIM), row),
        pl.BlockSpec((None, TM, cvw), row),
        pl.BlockSpec((None, HALO, cvw), lambda bi, i: (bi, jnp.maximum((i + off) * hb - 1, 0), 0)),
        pl.BlockSpec((None, HALO, cvw), lambda bi, i: (bi, jnp.minimum((i + off + 1) * hb, r // HALO - 1), 0)),
        full((1, RW_DIM)), full((1, RW_DIM)), full((1, RW_DIM)),
        full((3, CV_DIM)),
        pl.BlockSpec((None, D_MODEL, D_MODEL), lambda bi, i: (layer, 0, 0)),
    ]


def _out(stream, mod, p_rw, y_f, y_b, y_at, p_cv, r_k, ln_g, ln_b, cv_w, w_out, layer, final_g, nct, ctx_len,
         final):
    x_ops, split, b, r = _stream_operands(stream, 1)
    nt = r // TM
    off = nct if final else 0
    x_specs = _stream_specs(split, 1, nct, off)
    return pl.pallas_call(
        functools.partial(_out_kernel, nx=len(x_specs), tile_off=off, ctx_len=ctx_len, rows=r, final=final),
        grid=(b, nt - off),
        in_specs=x_specs + _out_in_specs(b, r, off, nct, p_cv.shape[-1], layer)
        + [pl.BlockSpec((1, D_MODEL), lambda bi, i: (0, 0))],
        out_specs=pl.BlockSpec((None, TM, D_MODEL), lambda bi, i: (bi, i, 0)),
        out_shape=jax.ShapeDtypeStruct((b, r - off * TM, D_MODEL), F32),
        compiler_params=pltpu.CompilerParams(
            dimension_semantics=("parallel", "parallel"), vmem_limit_bytes=VMEM_LIMIT),
        name="out",
    )(*x_ops, mod, p_rw, y_f, y_b, y_at, p_cv, p_cv, p_cv, r_k, ln_g, ln_b, cv_w, w_out, final_g)


def _rope_tables(seq, ctx_len):
    t = jnp.arange(seq)
    nfreq = HEAD_DIM // 4
    inv = ROPE_THETA ** (-jnp.arange(nfreq, dtype=F32) / nfreq)
    ang = jnp.concatenate([(t // GRID_W).astype(F32)[:, None] * inv,
                           (t % GRID_W).astype(F32)[:, None] * inv], axis=-1)
    cos, sin = jnp.cos(ang), jnp.sin(ang)
    reps = LANE // HEAD_DIM
    cos_t = jnp.tile(jnp.concatenate([cos, cos], axis=-1), (1, reps))
    sin_t = jnp.tile(jnp.concatenate([-sin, sin], axis=-1), (1, reps))
    cos_t = jnp.concatenate([jnp.ones((ctx_len, LANE), F32), cos_t], axis=0)
    sin_t = jnp.concatenate([jnp.zeros((ctx_len, LANE), F32), sin_t], axis=0)
    return cos_t, sin_t


def kernel(x, c, ctx, c_ctx, w_ada, b_ada, norm_g, w_in, w_out, rw_w0, rw_w_up, rw_a0, rw_a_up,
           rw_k_k, rw_k_a, rw_r_k, rw_ln_g, rw_ln_b, at_sink, cv_w, final_g):
    b, seq, _ = x.shape
    ctx_len = ctx.shape[1]
    depth = w_in.shape[0]
    assert ctx_len % TM == 0 and seq % TM == 0 and b + 1 <= 8
    nct = ctx_len // TM

    xa = (x, ctx)
    cvec = jnp.zeros((8, D_MODEL), F32).at[0:b].set(c).at[b].set(c_ctx)
    mod = _ada(cvec, w_ada, b_ada)
    cos_t, sin_t = _rope_tables(seq, ctx_len)
    band = _attn_band(ctx_len)

    for l in range(depth):
        mod_l = mod[l].reshape(8, 1, 3 * D_MODEL)
        p_rw, p_q, p_kv, p_cv = _proj(xa, mod_l, norm_g[l].reshape(1, -1), w_in, l, cos_t, sin_t, ctx_len)
        y_f, y_b = _rwkv(p_rw, rw_w0[l], rw_w_up[l].reshape(2 * HEAD_DIM, RW_DIM), rw_a0[l],
                         rw_a_up[l].reshape(2 * HEAD_DIM, RW_DIM), rw_k_k[l].reshape(1, -1),
                         rw_k_a[l].reshape(1, -1), nct)
        y_at = _attn(p_q, p_kv, at_sink[l], band, ctx_len)
        xa = _out(xa, mod_l, p_rw, y_f, y_b, y_at, p_cv, rw_r_k[l].reshape(1, -1),
                  rw_ln_g[l].reshape(1, -1), rw_ln_b[l].reshape(1, -1), cv_w[l], w_out, l,
                  final_g.reshape(1, -1), nct, ctx_len, final=(l == depth - 1))
    return xa
```

```python
import functools

import jax
import jax.numpy as jnp
from jax import lax
from jax.experimental import pallas as pl
from jax.experimental.pallas import tpu as pltpu

F32 = jnp.float32
BF16 = jnp.bfloat16

D_MODEL = 1024
HEAD_DIM = 64
GRID_W = 64
ROPE_THETA = 10000.0
RW_DIM = 384
AT_DIM = 384
CV_DIM = 256
WINDOW = 128
NORM_EPS = 1e-6
LNX_EPS = 64e-5
MASK_VALUE = -1e30
DECAY_SCALE = 0.6065306597126334
LOG2E = 1.4426950408889634
Q_SCALE = HEAD_DIM ** -0.5 * LOG2E

COL_RW = (0, 1792)
COL_Q = (1792, 2176)
COL_KV = (2176, 2432)
COL_CV = (2432, 3840)
D_IN = 3840

TM = 256
PROJ_TILES = (768, 512, 256)
CHUNK = 64
QB = 128
LANE = 128
HALO = 16
GROUP_W = 256
VMEM_LIMIT = 56 * 1024 * 1024


def _mm(a, b):
    return jnp.dot(a, b, preferred_element_type=F32)


def _mm_nt(a, b):
    return lax.dot_general(a, b, (((1,), (1,)), ((), ())), preferred_element_type=F32)


def _dot_hi(a, b):
    return jnp.dot(a, b, preferred_element_type=F32, precision=lax.Precision.HIGHEST)


def _dot_split(a, b, passes=2, left_exact=False):
    acc = None
    rem = a if not left_exact else b
    for p in range(passes):
        hi = rem.astype(BF16)
        t = jnp.dot(a, hi, preferred_element_type=F32) if left_exact else jnp.dot(hi, b, preferred_element_type=F32)
        acc = t if acc is None else acc + t
        if p + 1 < passes:
            rem = rem - hi.astype(F32)
    return acc


def _sigmoid(x):
    return 1.0 / (1.0 + jnp.exp(-x))


def _head_ones(n, scale=1.0):
    r = lax.broadcasted_iota(jnp.int32, (n, n), 0) // HEAD_DIM
    c = lax.broadcasted_iota(jnp.int32, (n, n), 1) // HEAD_DIM
    return jnp.where(r == c, scale, 0.0).astype(BF16)


def _head_sum(x, scale=1.0, passes=2):
    parts = []
    for lo in range(0, x.shape[1], GROUP_W):
        hi = min(lo + GROUP_W, x.shape[1])
        parts.append(_dot_split(x[:, lo:hi], _head_ones(hi - lo, scale), passes))
    return jnp.concatenate(parts, axis=1)


def _ada_kernel(c_ref, w_ref, b_ref, o_ref):
    c = c_ref[...]
    o_ref[...] = _dot_hi(c * _sigmoid(c), w_ref[...]) + b_ref[...]


def _ada(cvec, w_ada, b_ada):
    depth = w_ada.shape[0]
    tn = 1024
    return pl.pallas_call(
        _ada_kernel,
        grid=(depth, 3 * D_MODEL // tn),
        in_specs=[
            pl.BlockSpec((8, D_MODEL), lambda l, n: (0, 0)),
            pl.BlockSpec((None, D_MODEL, tn), lambda l, n: (l, 0, n)),
            pl.BlockSpec((None, 1, tn), lambda l, n: (l, 0, n)),
        ],
        out_specs=pl.BlockSpec((None, 8, tn), lambda l, n: (l, 0, n)),
        out_shape=jax.ShapeDtypeStruct((depth, 8, 3 * D_MODEL), F32),
        compiler_params=pltpu.CompilerParams(vmem_limit_bytes=VMEM_LIMIT),
        name="ada",
    )(cvec, w_ada, b_ada.reshape(depth, 1, 3 * D_MODEL))


def _stream_tile(x_refs, tile0, nct):
    if len(x_refs) == 1:
        return x_refs[0][...]
    nsub = len(x_refs) // 2
    blocks = [jnp.where(tile0 + k < nct, x_refs[nsub + k][...], x_refs[k][...]) for k in range(nsub)]
    return blocks[0] if nsub == 1 else jnp.concatenate(blocks, axis=0)


def _stream_specs(split, nsub, nct, off=0):
    if not split:
        return [pl.BlockSpec((None, nsub * TM, D_MODEL), lambda bi, i: (bi, i + off, 0))]
    lat = [pl.BlockSpec((None, TM, D_MODEL),
                        lambda bi, i, k=k: (bi, jnp.maximum((i + off) * nsub + k - nct, 0), 0)) for k in range(nsub)]
    ctx = [pl.BlockSpec((None, TM, D_MODEL),
                        lambda bi, i, k=k: (bi, jnp.minimum((i + off) * nsub + k, nct - 1), 0)) for k in range(nsub)]
    return lat + ctx


def _proj_kernel(*refs, ctx_len, nx):
    x_refs, (modb_ref, modc_ref, g_ref, w_ref, cos_ref, sin_ref, rw_ref, q_ref, kv_ref, cv_ref) = refs[:nx], refs[nx:]
    tm = rw_ref.shape[0]
    i = pl.program_id(1)
    x = _stream_tile(x_refs, i * (tm // TM), ctx_len // TM)
    _proj_compute(x, i * tm, modb_ref, modc_ref, g_ref, w_ref, cos_ref, sin_ref,
                  rw_ref, q_ref, kv_ref, cv_ref, ctx_len)


def _proj_compute(x, row0, modb_ref, modc_ref, g_ref, w_ref, cos_ref, sin_ref, rw_ref, q_ref, kv_ref, cv_ref,
                  ctx_len):
    tm = x.shape[0]
    ms = jnp.mean(x * x, axis=-1, keepdims=True)
    y = x * lax.rsqrt(ms + NORM_EPS) * g_ref[...]
    mb = modb_ref[...]
    mc = modc_ref[...]
    blocks = []
    for k in range(tm // TM):
        is_ctx = row0 + k * TM < ctx_len
        shift = jnp.where(is_ctx, mc[:, 0:D_MODEL], mb[:, 0:D_MODEL])
        gain = 1.0 + jnp.where(is_ctx, mc[:, D_MODEL:2 * D_MODEL], mb[:, D_MODEL:2 * D_MODEL])
        blocks.append(y[k * TM:(k + 1) * TM] * gain + shift)
    h = blocks[0] if len(blocks) == 1 else jnp.concatenate(blocks, axis=0)

    def seg(cols):
        return jnp.dot(h, w_ref[:, cols[0]:cols[1]], preferred_element_type=F32)

    cos = cos_ref[...]
    sin = sin_ref[...]
    lane = lax.broadcasted_iota(jnp.int32, (tm, LANE), 1)
    first_half = (lane % HEAD_DIM) < (HEAD_DIM // 2)
    low_head = lane < HEAD_DIM

    def rope(t):
        rot = jnp.where(first_half, pltpu.roll(t, LANE - 32, 1), pltpu.roll(t, 32, 1))
        return t * cos + rot * sin

    def dup(t):
        sw = pltpu.roll(t, HEAD_DIM, 1)
        return jnp.where(low_head, t, sw), jnp.where(low_head, sw, t)

    q = seg(COL_Q)
    q_ref[...] = jnp.concatenate(
        [rope(q[:, LANE * j:LANE * (j + 1)]) * Q_SCALE for j in range(AT_DIM // LANE)],
        axis=1).astype(BF16)
    kv = seg(COL_KV)
    k0, k1 = dup(rope(kv[:, 0:LANE]))
    v0, v1 = dup(kv[:, LANE:2 * LANE])
    kv_ref[...] = jnp.concatenate([k0, k1, v0, v1], axis=1).astype(BF16)
    rw_ref[...] = seg(COL_RW).astype(BF16)
    cv_ref[...] = seg(COL_CV).astype(BF16)


def _stream_operands(stream, nsub):
    if isinstance(stream, tuple):
        x, ctx = stream
        return [x] * nsub + [ctx] * nsub, True, x.shape[0], x.shape[1] + ctx.shape[1]
    return [stream], False, stream.shape[0], stream.shape[1]


def _proj(stream, mod, norm_g, w_in, layer, cos_t, sin_t, ctx_len):
    _, split, b, r = _stream_operands(stream, 1)
    tm = next(t for t in PROJ_TILES if r % t == 0)
    nsub = tm // TM
    x_ops = _stream_operands(stream, nsub)[0]
    x_specs = _stream_specs(split, nsub, ctx_len // TM)
    row = lambda bi, i: (bi, i, 0)
    widths = (COL_RW[1] - COL_RW[0], AT_DIM, 4 * LANE, COL_CV[1] - COL_CV[0])
    return pl.pallas_call(
        functools.partial(_proj_kernel, ctx_len=ctx_len, nx=len(x_specs)),
        grid=(b, r // tm),
        in_specs=x_specs + [
            pl.BlockSpec((None, 1, 3 * D_MODEL), lambda bi, i: (bi, 0, 0)),
            pl.BlockSpec((None, 1, 3 * D_MODEL), lambda bi, i: (b, 0, 0)),
            pl.BlockSpec((1, D_MODEL), lambda bi, i: (0, 0)),
            pl.BlockSpec((None, D_MODEL, D_IN), lambda bi, i: (layer, 0, 0), pipeline_mode=pl.Buffered(1)),
            pl.BlockSpec((tm, LANE), lambda bi, i: (i, 0)),
            pl.BlockSpec((tm, LANE), lambda bi, i: (i, 0)),
        ],
        out_specs=[pl.BlockSpec((None, tm, w), row) for w in widths],
        out_shape=[jax.ShapeDtypeStruct((b, r, w), BF16) for w in widths],
        compiler_params=pltpu.CompilerParams(
            dimension_semantics=("parallel", "parallel"), vmem_limit_bytes=VMEM_LIMIT),
        name="proj",
    )(*x_ops, mod, mod, norm_g, w_in, cos_t, sin_t)


_RW_FIELDS = ("at", "rt", "kt", "bt", "kh", "bh", "v")


def _group_masks(rev_lo, rev_hi):
    w = GROUP_W
    row = lax.broadcasted_iota(jnp.int32, (CHUNK, w), 0)
    lane = lax.broadcasted_iota(jnp.int32, (CHUNK, w), 1)
    s = lane % CHUNK
    x = row ^ s
    if rev_lo == rev_hi:
        before = (row - s) if rev_lo else (s - row)
    else:
        sign = jnp.where(lane >= LANE, -1 if rev_hi else 1, -1 if rev_lo else 1)
        before = (s - row) * sign
    m = {
        "strict": before < 0,
        "incl": before <= 0,
        "eye": s == row,
        "lvl1": (x >> 1) == 0,
        "heads": [(lane // HEAD_DIM) == h for h in range(w // HEAD_DIM)],
    }
    for lg in range(1, 6):
        m["lvl%d" % (1 << lg)] = (x >> lg) == 1
    return m


def _bd(x, m):
    return jnp.concatenate([jnp.where(hm, x, 0.0) for hm in m["heads"]], axis=0)


def _head_transpose(x):
    xt = x.T
    return jnp.concatenate([xt[h * HEAD_DIM:(h + 1) * HEAD_DIM] for h in range(x.shape[1] // HEAD_DIM)], axis=1)


def _chunk_pieces(ch, side=()):
    def advance():
        for gen in side:
            next(gen, None)

    n = range(len(ch))
    ms = [c["m"] for c in ch]
    advance()
    ar = [jnp.concatenate([c["at"], c["rt"]], axis=0) for c in ch]
    gb = [_mm(ar[i], _bd(_head_transpose(ch[i]["bt"]), ms[i])) for i in n]
    gk = [_mm(ar[i], _bd(_head_transpose(ch[i]["kt"]), ms[i])) for i in n]
    lo = [jnp.where(ms[i]["strict"], gb[i][:CHUNK], 0.0) for i in n]
    aak = [jnp.where(ms[i]["strict"], gk[i][:CHUNK], 0.0) for i in n]
    arb = [jnp.where(ms[i]["incl"], gb[i][CHUNK:], 0.0) for i in n]
    ark = [jnp.where(ms[i]["incl"], gk[i][CHUNK:], 0.0) for i in n]

    t = [jnp.where(ms[i]["eye"], 1.0, 0.0) - jnp.where(ms[i]["lvl1"], lo[i], 0.0) for i in n]
    for lvl in (2, 4, 8, 16, 32):
        x = [_mm(t[i], _bd(jnp.where(ms[i]["lvl%d" % lvl], lo[i], 0.0), ms[i])) for i in n]
        t = [t[i] - _mm(x[i], _bd(t[i], ms[i])) for i in n]
        advance()

    c2 = 2 * CHUNK
    bht = [_head_transpose(c["bh"]) for c in ch]
    kht = [_head_transpose(c["kh"]) for c in ch]
    av = [_mm(jnp.concatenate([aak[i], ark[i], kht[i]], axis=0), _bd(ch[i]["v"], ms[i])) for i in n]
    z = [_mm(jnp.concatenate([arb[i], bht[i]], axis=0), _bd(t[i], ms[i])) for i in n]
    pw = [_mm(z[i], _bd(ch[i]["at"], ms[i])) for i in n]
    pu = [_mm(z[i], _bd(av[i][:CHUNK], ms[i])) for i in n]
    qh = [ch[i]["rt"] - pw[i][:CHUNK] for i in n]
    yi = [av[i][CHUNK:c2] - pu[i][:CHUNK] for i in n]
    mm = [jnp.where(ms[i]["eye"], ch[i]["pe"], 0.0) - pw[i][CHUNK:] for i in n]
    nn = [av[i][c2:] - pu[i][CHUNK:] for i in n]
    return qh, yi, mm, nn


_RW_SLOT_FIELDS = _RW_FIELDS + ("pe",)
_RW_LAYOUT = (((0, 0, 256),), ((0, 256, 384), (1, 0, 128)), ((1, 128, 384),))


def _rwkv_prep(x_ref, d, w0_ref, wup_ref, a0_ref, aup_ref, kk_ref, ka_ref, dst):
    nchunk = TM // CHUNK
    rev = d == 1
    rt_i = lax.broadcasted_iota(jnp.int32, (TM, TM), 0)
    ct_i = lax.broadcasted_iota(jnp.int32, (TM, TM), 1)
    same_chunk = (rt_i // CHUNK) == (ct_i // CHUNK)
    lane = lax.broadcasted_iota(jnp.int32, (TM, LANE), 1)
    r = x_ref[:, 0:384].astype(F32)
    k = x_ref[:, 384:768].astype(F32)
    dmask = (lane // HEAD_DIM) == d
    tw = jnp.where(dmask, jnp.tanh(x_ref[:, 1536:1664].astype(F32)), 0.0)
    ta = jnp.where(dmask, x_ref[:, 1664:1792].astype(F32), 0.0)
    lw = w0_ref[d:d + 1, :] + _mm(tw, wup_ref[...])
    la = a0_ref[d:d + 1, :] + _mm(ta, aup_ref[...])
    kk = k * kk_ref[...]
    ss = _head_sum(kk * kk, passes=1)
    yield
    logw = -(DECAY_SCALE * LOG2E) * _sigmoid(lw)
    tri = jnp.where(same_chunk & ((ct_i >= rt_i) if rev else (ct_i <= rt_i)), 1.0, 0.0).astype(BF16)
    cl = _dot_split(tri, logw, left_exact=True)
    yield
    a = _sigmoid(la)
    kk = kk * lax.rsqrt(jnp.maximum(ss, 1e-12))
    kd = k * (1.0 + (a - 1.0) * ka_ref[...])
    beta = kk * a
    ends = [(c * CHUNK) if rev else (c * CHUNK + CHUNK - 1) for c in range(nchunk)]
    pe = jnp.concatenate([jnp.broadcast_to(jnp.exp2(cl[e:e + 1, :]), (CHUNK, RW_DIM)) for e in ends], axis=0)
    inv = jnp.exp2(-cl)
    kt = kd * inv
    bt = beta * inv
    dst["at"][...] = kk * jnp.exp2(cl - logw)
    dst["rt"][...] = r * jnp.exp2(cl)
    dst["kt"][...] = kt
    dst["bt"][...] = bt
    dst["kh"][...] = kt * pe
    dst["bh"][...] = bt * pe
    dst["v"][...] = x_ref[:, 768:1152].astype(F32)
    dst["pe"][...] = pe


def _rwkv_chains(src, masks, groups):
    nchunk = TM // CHUNK
    chains = []
    for step in range(nchunk):
        rows = (slice(step * CHUNK, (step + 1) * CHUNK),
                slice((nchunk - 1 - step) * CHUNK, (nchunk - step) * CHUNK))
        for g in groups:
            parts = _RW_LAYOUT[g]

            def gather(name):
                cols = [src[d][name][rows[d], lo:hi] for d, lo, hi in parts]
                return cols[0] if len(cols) == 1 else jnp.concatenate(cols, axis=1)
            ch = {name: gather(name) for name in _RW_FIELDS}
            ch.update(pe=gather("pe")[0:1, :], m=masks[g], g=g, step=step,
                      dst=[(d, rows[d], lo, hi) for d, lo, hi in parts])
            chains.append(ch)
    return chains


def _rwkv_state_pass(chains, pieces, yf_ref, yb_ref, h_refs):
    qh, yi, mm, nn = pieces
    y_refs = (yf_ref, yb_ref)
    h = [h_ref[...] for h_ref in h_refs]
    for step in range(TM // CHUNK):
        idx = [i for i, c in enumerate(chains) if c["step"] == step]
        res = {i: _mm(jnp.concatenate([qh[i], mm[i]], axis=0), _bd(h[chains[i]["g"]], chains[i]["m"]))
               for i in idx}
        for i in idx:
            ys = res[i][:CHUNK] + yi[i]
            off = 0
            for d, rws, lo, hi in chains[i]["dst"]:
                y_refs[d][rws, lo:hi] = ys[:, off:off + hi - lo].astype(yf_ref.dtype)
                off += hi - lo
            h[chains[i]["g"]] = res[i][CHUNK:] + nn[i]
    for h_ref, hv in zip(h_refs, h):
        h_ref[...] = hv


def _rwkv_kernel(xf_ref, xb_ref, w0_ref, wup_ref, a0_ref, aup_ref, kk_ref, ka_ref,
                 yf_ref, yb_ref, *scratch):
    ngroup = len(_RW_LAYOUT)
    h_refs = scratch[:ngroup]
    nf = len(_RW_SLOT_FIELDS)
    prepared = [dict(zip(_RW_SLOT_FIELDS, scratch[ngroup + d * nf:ngroup + (d + 1) * nf])) for d in range(2)]
    params = (w0_ref, wup_ref, a0_ref, aup_ref, kk_ref, ka_ref)

    @pl.when(pl.program_id(1) == 0)
    def _():
        for ref in h_refs:
            ref[...] = jnp.zeros_like(ref)

    masks = [_group_masks(False, False), _group_masks(False, True), _group_masks(True, True)]
    for _ in _rwkv_prep(xf_ref, 0, *params, prepared[0]):
        pass
    prep_b = _rwkv_prep(xb_ref, 1, *params, prepared[1])
    chains_a = _rwkv_chains(prepared, masks, (0,))
    pieces_a = _chunk_pieces(chains_a, (prep_b,))
    for _ in prep_b:
        pass
    chains_b = _rwkv_chains(prepared, masks, (1, 2))
    pieces_b = _chunk_pieces(chains_b)
    pieces = [a + b for a, b in zip(pieces_a, pieces_b)]
    _rwkv_state_pass(chains_a + chains_b, pieces, yf_ref, yb_ref, h_refs)


def _bwd_tile(i, nct, nt):
    return jnp.where(i < nct, nct - 1 - i, nt - 1 - (i - nct))


def _rwkv(p_rw, w0, wup, a0, aup, k_k, k_a, nct):
    b, r, wcols = p_rw.shape
    nt = r // TM
    fwd = lambda bi, j: (bi, j, 0)
    bwd = lambda bi, j: (bi, _bwd_tile(j, nct, nt), 0)
    full = lambda shape: pl.BlockSpec(shape, lambda bi, j: (0,) * len(shape))
    scratch = [pltpu.VMEM((HEAD_DIM, GROUP_W), F32) for _ in _RW_LAYOUT]
    scratch += [pltpu.VMEM((TM, RW_DIM), F32) for _ in range(2 * len(_RW_SLOT_FIELDS))]
    return pl.pallas_call(
        _rwkv_kernel,
        grid=(b, nt),
        in_specs=[
            pl.BlockSpec((None, TM, wcols), fwd),
            pl.BlockSpec((None, TM, wcols), bwd),
            full((2, RW_DIM)), full((2 * HEAD_DIM, RW_DIM)),
            full((2, RW_DIM)), full((2 * HEAD_DIM, RW_DIM)),
            full((1, RW_DIM)), full((1, RW_DIM)),
        ],
        out_specs=[pl.BlockSpec((None, TM, RW_DIM), fwd), pl.BlockSpec((None, TM, RW_DIM), bwd)],
        out_shape=[jax.ShapeDtypeStruct((b, r, RW_DIM), BF16)] * 2,
        scratch_shapes=scratch,
        compiler_params=pltpu.CompilerParams(
            dimension_semantics=("parallel", "arbitrary"), vmem_limit_bytes=VMEM_LIMIT),
        name="rwkv",
    )(p_rw, p_rw, w0, wup, a0, aup, k_k, k_a)


def _attn_compute(i, sink_ref, band_ref, q_ref, kp_ref, kc_ref, kn_ref, kx_ref, nct, nt):
    latent = i >= nct
    nloc = 3 * QB
    nkeys = nloc + kx_ref.shape[0]

    lane = lax.broadcasted_iota(jnp.int32, (QB, LANE), 1)
    low = lane < HEAD_DIM
    low_m = jnp.where(low, 1.0, 0.0).astype(BF16)
    high_m = jnp.where(low, 0.0, 1.0).astype(BF16)
    col = lax.broadcasted_iota(jnp.int32, (1, nkeys), 1)
    hrow = lax.broadcasted_iota(jnp.int32, (3 * QB, 1), 0) // QB
    band = band_ref[...]
    head_src = (((0, low_m), (0, high_m), (1, low_m)), ((1, high_m), (2, low_m), (2, high_m)))

    qs, kall, vall, bias, sk = [], [], [], [], []
    for qb in range(TM // QB):
        qrows = slice(qb * QB, (qb + 1) * QB)
        qt = [q_ref[qrows, LANE * t:LANE * (t + 1)] for t in range(AT_DIM // LANE)]
        halves = [(kc_ref, slice(0, QB)), (kc_ref, slice(QB, 2 * QB))]
        if qb == 0:
            blocks = [(kp_ref, slice(0, QB))] + halves
            first_ok, last_ok = i > nct, True
        else:
            blocks = halves + [(kn_ref, slice(0, QB))]
            first_ok, last_ok = True, i < nt - 1
        lo_c = jnp.where(jnp.logical_and(latent, first_ok), 0, jnp.where(latent, QB, nloc))
        hi_c = jnp.where(last_ok, nloc, 2 * QB)
        dead = jnp.logical_and(col < nloc, jnp.logical_or(col < lo_c, col >= hi_c))
        bias_qb = band + jnp.where(dead, MASK_VALUE, 0.0)
        for g in range(2):
            qs.append(jnp.concatenate([qt[t] * hm for t, hm in head_src[g]], axis=0))
            kcols = slice(LANE * g, LANE * (g + 1))
            vcols = slice(2 * LANE + LANE * g, 2 * LANE + LANE * (g + 1))
            kall.append(jnp.concatenate([ref[rs, kcols] for ref, rs in blocks] + [kx_ref[:, kcols]], axis=0))
            vall.append(jnp.concatenate([ref[rs, vcols] for ref, rs in blocks] + [kx_ref[:, vcols]], axis=0))
            bias.append(bias_qb)
            sk.append(LOG2E * jnp.where(hrow == 0, sink_ref[3 * g],
                                        jnp.where(hrow == 1, sink_ref[3 * g + 1], sink_ref[3 * g + 2])))

    n = range(len(qs))
    s = [_mm_nt(qs[c], kall[c]) + bias[c] for c in n]
    mx = [jnp.maximum(jnp.max(s[c], axis=-1, keepdims=True), sk[c]) for c in n]
    p = [jnp.exp2(s[c] - mx[c]) for c in n]
    den = [jnp.sum(p[c], axis=-1, keepdims=True) + jnp.exp2(sk[c] - mx[c]) for c in n]
    o = [_mm(p[c].astype(BF16), vall[c]) / den[c] for c in n]

    blocks = []
    for qb in range(TM // QB):
        o0, o1 = o[2 * qb], o[2 * qb + 1]
        blocks.append(jnp.concatenate([
            jnp.where(low, o0[0:QB], o0[QB:2 * QB]),
            jnp.where(low, o0[2 * QB:3 * QB], o1[0:QB]),
            jnp.where(low, o1[QB:2 * QB], o1[2 * QB:3 * QB]),
        ], axis=1))
    return jnp.concatenate(blocks, axis=0)


def _attn_band(ctx_len):
    t = jnp.arange(3 * QB)[:, None] % QB
    col = jnp.arange(3 * QB + ctx_len)[None, :]
    ok = ((col >= t) & (col <= t + 2 * WINDOW)) | (col >= 3 * QB)
    return jnp.where(ok, 0.0, MASK_VALUE).astype(F32)


def _attn_specs(r, off, ctx_len, band_shape, kvw):
    nct = ctx_len // TM
    per = TM // QB
    nb = r // QB
    clamp = lambda j: jnp.clip(j, nct * per, nb - 1)
    return [
        pl.BlockSpec(memory_space=pltpu.SMEM),
        pl.BlockSpec(band_shape, lambda bi, i: (0, 0)),
        pl.BlockSpec((None, TM, AT_DIM), lambda bi, i: (bi, i + off, 0)),
        pl.BlockSpec((None, QB, kvw), lambda bi, i: (bi, clamp((i + off) * per - 1), 0)),
        pl.BlockSpec((None, TM, kvw), lambda bi, i: (bi, i + off, 0)),
        pl.BlockSpec((None, QB, kvw), lambda bi, i: (bi, clamp((i + off + 1) * per), 0)),
        pl.BlockSpec((None, ctx_len, kvw), lambda bi, i: (bi, 0, 0)),
    ]


def _out_kernel(*refs, nx, tile_off, ctx_len, rows, final):
    x_refs, attn_refs, rest = refs[:nx], refs[nx:nx + 7], refs[nx + 7:]
    fg_ref, o_ref = rest[-2:]
    tile = pl.program_id(1) + tile_off
    x = _stream_tile(x_refs, tile, ctx_len // TM)
    partial, at_gate = _mix_compute(tile * TM, *rest[1:-2], ctx_len, rows)
    y_at = _attn_compute(tile, *attn_refs, ctx_len // TM, rows // TM)
    mod_ref, wout_ref = rest[0], rest[-3]
    acc = partial + _mm(y_at * at_gate, wout_ref[384:768, :])
    xn = x + mod_ref[:, 2 * D_MODEL:3 * D_MODEL] * acc
    if final:
        ms = jnp.mean(xn * xn, axis=-1, keepdims=True)
        xn = xn * lax.rsqrt(ms + NORM_EPS) * fg_ref[...]
    o_ref[...] = xn


def _mix_compute(row0, rw_ref, yf_ref, yb_ref, cv_ref, cvp_ref, cvn_ref,
                 rk_ref, lng_ref, lnb_ref, cvw_ref, wout_ref, ctx_len, rows):
    r = rw_ref[:, 0:384].astype(F32)
    k = rw_ref[:, 384:768].astype(F32)
    v = rw_ref[:, 768:1152].astype(F32)
    g = rw_ref[:, 1152:1536].astype(F32)

    y = yf_ref[...].astype(F32) + yb_ref[...].astype(F32)
    dev = y - _head_sum(y, 1.0 / HEAD_DIM)
    var = _head_sum(dev * dev, 1.0 / HEAD_DIM, passes=1)
    yn = dev * lax.rsqrt(var + LNX_EPS) * lng_ref[...] + lnb_ref[...]
    bonus = _head_sum(r * k * rk_ref[...], passes=1) * v
    y_rw = (yn + bonus) * (g * _sigmoid(g))

    g_at = cv_ref[:, 0:384].astype(F32)
    at_gate = g_at * _sigmoid(g_at)

    bg = cv_ref[:, 384:640].astype(F32)
    u = cv_ref[:, 640:896].astype(F32) * cv_ref[:, 896:1152].astype(F32)
    g_cv = cv_ref[:, 1152:1408].astype(F32)
    prev_ok = jnp.logical_and(row0 != 0, row0 != ctx_len)
    next_ok = jnp.logical_and(row0 + TM != ctx_len, row0 + TM != rows)
    pu = cvp_ref[:, 640:896].astype(F32) * cvp_ref[:, 896:1152].astype(F32)
    nu = cvn_ref[:, 640:896].astype(F32) * cvn_ref[:, 896:1152].astype(F32)
    u_prev = jnp.where(prev_ok, pu[HALO - 1:HALO, :], 0.0)
    u_next = jnp.where(next_ok, nu[0:1, :], 0.0)
    ridx = lax.broadcasted_iota(jnp.int32, (TM, CV_DIM), 0)
    um = jnp.where(ridx == 0, u_prev, pltpu.roll(u, 1, 0))
    up = jnp.where(ridx == TM - 1, u_next, pltpu.roll(u, TM - 1, 0))
    conv = um * cvw_ref[0:1, :] + u * cvw_ref[1:2, :] + up * cvw_ref[2:3, :]
    y_cv = bg * conv * (g_cv * _sigmoid(g_cv))

    return _mm(y_rw, wout_ref[0:384, :]) + _mm(y_cv, wout_ref[768:1024, :]), at_gate


def _out_in_specs(b, r, off, nct, cvw, layer):
    hb = TM // HALO
    row = lambda bi, i: (bi, i + off, 0)
    full = lambda shape: pl.BlockSpec(shape, lambda bi, i: (0,) * len(shape))
    return [
        pl.BlockSpec((None, 1, 3 * D_MODEL), lambda bi, i: (jnp.where(i + off < nct, b, bi), 0, 0)),
        pl.BlockSpec((None, TM, 1536), row),
        pl.BlockSpec((None, TM, RW_DIM), row),
        pl.BlockSpec((None, TM, RW_DIM), row),
        pl.BlockSpec((None, TM, cvw), row),
        pl.BlockSpec((None, HALO, cvw), lambda bi, i: (bi, jnp.maximum((i + off) * hb - 1, 0), 0)),
        pl.BlockSpec((None, HALO, cvw), lambda bi, i: (bi, jnp.minimum((i + off + 1) * hb, r // HALO - 1), 0)),
        full((1, RW_DIM)), full((1, RW_DIM)), full((1, RW_DIM)),
        full((3, CV_DIM)),
        pl.BlockSpec((None, D_MODEL, D_MODEL), lambda bi, i: (layer, 0, 0)),
    ]


def _out(stream, p_q, p_kv, sink, band, mod, p_rw, y_f, y_b, p_cv, r_k, ln_g, ln_b, cv_w, w_out, layer, final_g,
         nct, ctx_len, final):
    x_ops, split, b, r = _stream_operands(stream, 1)
    nt = r // TM
    off = nct if final else 0
    x_specs = _stream_specs(split, 1, nct, off)
    return pl.pallas_call(
        functools.partial(_out_kernel, nx=len(x_specs), tile_off=off, ctx_len=ctx_len, rows=r, final=final),
        grid=(b, nt - off),
        in_specs=x_specs + _attn_specs(r, off, ctx_len, band.shape, p_kv.shape[-1])
        + _out_in_specs(b, r, off, nct, p_cv.shape[-1], layer)
        + [pl.BlockSpec((1, D_MODEL), lambda bi, i: (0, 0))],
        out_specs=pl.BlockSpec((None, TM, D_MODEL), lambda bi, i: (bi, i, 0)),
        out_shape=jax.ShapeDtypeStruct((b, r - off * TM, D_MODEL), F32),
        compiler_params=pltpu.CompilerParams(
            dimension_semantics=("parallel", "parallel"), vmem_limit_bytes=VMEM_LIMIT),
        name="out",
    )(*x_ops, sink, band, p_q, p_kv, p_kv, p_kv, p_kv,
      mod, p_rw, y_f, y_b, p_cv, p_cv, p_cv, r_k, ln_g, ln_b, cv_w, w_out, final_g)


def _rope_tables(seq, ctx_len):
    t = jnp.arange(seq)
    nfreq = HEAD_DIM // 4
    inv = ROPE_THETA ** (-jnp.arange(nfreq, dtype=F32) / nfreq)
    ang = jnp.concatenate([(t // GRID_W).astype(F32)[:, None] * inv,
                           (t % GRID_W).astype(F32)[:, None] * inv], axis=-1)
    cos, sin = jnp.cos(ang), jnp.sin(ang)
    reps = LANE // HEAD_DIM
    cos_t = jnp.tile(jnp.concatenate([cos, cos], axis=-1), (1, reps))
    sin_t = jnp.tile(jnp.concatenate([-sin, sin], axis=-1), (1, reps))
    cos_t = jnp.concatenate([jnp.ones((ctx_len, LANE), F32), cos_t], axis=0)
    sin_t = jnp.concatenate([jnp.zeros((ctx_len, LANE), F32), sin_t], axis=0)
    return cos_t, sin_t


def kernel(x, c, ctx, c_ctx, w_ada, b_ada, norm_g, w_in, w_out, rw_w0, rw_w_up, rw_a0, rw_a_up,
           rw_k_k, rw_k_a, rw_r_k, rw_ln_g, rw_ln_b, at_sink, cv_w, final_g):
    b, seq, _ = x.shape
    ctx_len = ctx.shape[1]
    depth = w_in.shape[0]
    assert ctx_len % TM == 0 and seq % TM == 0 and b + 1 <= 8
    nct = ctx_len // TM

    xa = (x, ctx)
    cvec = jnp.zeros((8, D_MODEL), F32).at[0:b].set(c).at[b].set(c_ctx)
    mod = _ada(cvec, w_ada, b_ada)
    cos_t, sin_t = _rope_tables(seq, ctx_len)
    band = _attn_band(ctx_len)

    for l in range(depth):
        mod_l = mod[l].reshape(8, 1, 3 * D_MODEL)
        p_rw, p_q, p_kv, p_cv = _proj(xa, mod_l, norm_g[l].reshape(1, -1), w_in, l, cos_t, sin_t, ctx_len)
        y_f, y_b = _rwkv(p_rw, rw_w0[l], rw_w_up[l].reshape(2 * HEAD_DIM, RW_DIM), rw_a0[l],
                         rw_a_up[l].reshape(2 * HEAD_DIM, RW_DIM), rw_k_k[l].reshape(1, -1),
                         rw_k_a[l].reshape(1, -1), nct)
        xa = _out(xa, p_q, p_kv, at_sink[l], band, mod_l, p_rw, y_f, y_b, p_cv, rw_r_k[l].reshape(1, -1),
                  rw_ln_g[l].reshape(1, -1), rw_ln_b[l].reshape(1, -1), cv_w[l], w_out, l,
                  final_g.reshape(1, -1), nct, ctx_len, final=(l == depth - 1))
    return xa
```

```python
import functools

import jax
import jax.numpy as jnp
from jax import lax
from jax.experimental import pallas as pl
from jax.experimental.pallas import tpu as pltpu

F32 = jnp.float32
BF16 = jnp.bfloat16

D_MODEL = 1024
HEAD_DIM = 64
GRID_W = 64
ROPE_THETA = 10000.0
RW_DIM = 384
AT_DIM = 384
CV_DIM = 256
WINDOW = 128
NORM_EPS = 1e-6
LNX_EPS = 64e-5
MASK_VALUE = -1e30
DECAY_SCALE = 0.6065306597126334
LOG2E = 1.4426950408889634
Q_SCALE = HEAD_DIM ** -0.5 * LOG2E

COL_RW = (0, 1792)
COL_Q = (1792, 2176)
COL_KV = (2176, 2432)
COL_CV = (2432, 3840)
D_IN = 3840

TM = 256
PROJ_TILES = (768, 512, 256)
CHUNK = 64
RW_TILES = 3
QB = 128
LANE = 128
HALO = 16
GROUP_W = 256
VMEM_LIMIT = 56 * 1024 * 1024


def _mm(a, b):
    return jnp.dot(a, b, preferred_element_type=F32)


def _mm_nt(a, b):
    return lax.dot_general(a, b, (((1,), (1,)), ((), ())), preferred_element_type=F32)


def _dot_hi(a, b):
    return jnp.dot(a, b, preferred_element_type=F32, precision=lax.Precision.HIGHEST)


def _dot_split(a, b, passes=2, left_exact=False):
    acc = None
    rem = a if not left_exact else b
    for p in range(passes):
        hi = rem.astype(BF16)
        t = jnp.dot(a, hi, preferred_element_type=F32) if left_exact else jnp.dot(hi, b, preferred_element_type=F32)
        acc = t if acc is None else acc + t
        if p + 1 < passes:
            rem = rem - hi.astype(F32)
    return acc


def _sigmoid(x):
    return 1.0 / (1.0 + jnp.exp(-x))


def _head_ones(n, scale=1.0):
    r = lax.broadcasted_iota(jnp.int32, (n, n), 0) // HEAD_DIM
    c = lax.broadcasted_iota(jnp.int32, (n, n), 1) // HEAD_DIM
    return jnp.where(r == c, scale, 0.0).astype(BF16)


def _head_sum(x, scale=1.0, passes=2):
    parts = []
    for lo in range(0, x.shape[1], GROUP_W):
        hi = min(lo + GROUP_W, x.shape[1])
        parts.append(_dot_split(x[:, lo:hi], _head_ones(hi - lo, scale), passes))
    return jnp.concatenate(parts, axis=1)


def _ada_kernel(c_ref, w_ref, b_ref, o_ref):
    c = c_ref[...]
    o_ref[...] = _dot_hi(c * _sigmoid(c), w_ref[...]) + b_ref[...]


def _ada(cvec, w_ada, b_ada):
    depth = w_ada.shape[0]
    tn = 1024
    return pl.pallas_call(
        _ada_kernel,
        grid=(depth, 3 * D_MODEL // tn),
        in_specs=[
            pl.BlockSpec((8, D_MODEL), lambda l, n: (0, 0)),
            pl.BlockSpec((None, D_MODEL, tn), lambda l, n: (l, 0, n)),
            pl.BlockSpec((None, 1, tn), lambda l, n: (l, 0, n)),
        ],
        out_specs=pl.BlockSpec((None, 8, tn), lambda l, n: (l, 0, n)),
        out_shape=jax.ShapeDtypeStruct((depth, 8, 3 * D_MODEL), F32),
        compiler_params=pltpu.CompilerParams(vmem_limit_bytes=VMEM_LIMIT),
        name="ada",
    )(cvec, w_ada, b_ada.reshape(depth, 1, 3 * D_MODEL))


def _stream_tile(x_refs, tile0, nct):
    if len(x_refs) == 1:
        return x_refs[0][...]
    nsub = len(x_refs) // 2
    blocks = [jnp.where(tile0 + k < nct, x_refs[nsub + k][...], x_refs[k][...]) for k in range(nsub)]
    return blocks[0] if nsub == 1 else jnp.concatenate(blocks, axis=0)


def _stream_specs(split, nsub, nct, off=0):
    if not split:
        return [pl.BlockSpec((None, nsub * TM, D_MODEL), lambda bi, i: (bi, i + off, 0))]
    lat = [pl.BlockSpec((None, TM, D_MODEL),
                        lambda bi, i, k=k: (bi, jnp.maximum((i + off) * nsub + k - nct, 0), 0)) for k in range(nsub)]
    ctx = [pl.BlockSpec((None, TM, D_MODEL),
                        lambda bi, i, k=k: (bi, jnp.minimum((i + off) * nsub + k, nct - 1), 0)) for k in range(nsub)]
    return lat + ctx


def _proj_kernel(*refs, ctx_len, nx):
    x_refs, (modb_ref, modc_ref, g_ref, w_ref, cos_ref, sin_ref, rw_ref, q_ref, kv_ref, cv_ref) = refs[:nx], refs[nx:]
    tm = rw_ref.shape[0]
    i = pl.program_id(1)
    x = _stream_tile(x_refs, i * (tm // TM), ctx_len // TM)
    _proj_compute(x, i * tm, modb_ref, modc_ref, g_ref, w_ref, cos_ref, sin_ref,
                  rw_ref, q_ref, kv_ref, cv_ref, ctx_len)


def _proj_compute(x, row0, modb_ref, modc_ref, g_ref, w_ref, cos_ref, sin_ref, rw_ref, q_ref, kv_ref, cv_ref,
                  ctx_len):
    tm = x.shape[0]
    ms = jnp.mean(x * x, axis=-1, keepdims=True)
    y = x * lax.rsqrt(ms + NORM_EPS) * g_ref[...]
    mb = modb_ref[...]
    mc = modc_ref[...]
    blocks = []
    for k in range(tm // TM):
        is_ctx = row0 + k * TM < ctx_len
        shift = jnp.where(is_ctx, mc[:, 0:D_MODEL], mb[:, 0:D_MODEL])
        gain = 1.0 + jnp.where(is_ctx, mc[:, D_MODEL:2 * D_MODEL], mb[:, D_MODEL:2 * D_MODEL])
        blocks.append(y[k * TM:(k + 1) * TM] * gain + shift)
    h = blocks[0] if len(blocks) == 1 else jnp.concatenate(blocks, axis=0)

    def seg(cols):
        return jnp.dot(h, w_ref[:, cols[0]:cols[1]], preferred_element_type=F32)

    cos = cos_ref[...]
    sin = sin_ref[...]
    lane = lax.broadcasted_iota(jnp.int32, (tm, LANE), 1)
    first_half = (lane % HEAD_DIM) < (HEAD_DIM // 2)
    low_head = lane < HEAD_DIM

    def rope(t):
        rot = jnp.where(first_half, pltpu.roll(t, LANE - 32, 1), pltpu.roll(t, 32, 1))
        return t * cos + rot * sin

    def dup(t):
        sw = pltpu.roll(t, HEAD_DIM, 1)
        return jnp.where(low_head, t, sw), jnp.where(low_head, sw, t)

    q = seg(COL_Q)
    q_ref[...] = jnp.concatenate(
        [rope(q[:, LANE * j:LANE * (j + 1)]) * Q_SCALE for j in range(AT_DIM // LANE)],
        axis=1).astype(BF16)
    kv = seg(COL_KV)
    k0, k1 = dup(rope(kv[:, 0:LANE]))
    v0, v1 = dup(kv[:, LANE:2 * LANE])
    kv_ref[...] = jnp.concatenate([k0, k1, v0, v1], axis=1).astype(BF16)
    rw_ref[...] = seg(COL_RW).astype(BF16)
    cv_ref[...] = seg(COL_CV).astype(BF16)


def _stream_operands(stream, nsub):
    if isinstance(stream, tuple):
        x, ctx = stream
        return [x] * nsub + [ctx] * nsub, True, x.shape[0], x.shape[1] + ctx.shape[1]
    return [stream], False, stream.shape[0], stream.shape[1]


def _proj(stream, mod, norm_g, w_in, layer, cos_t, sin_t, ctx_len):
    _, split, b, r = _stream_operands(stream, 1)
    tm = next(t for t in PROJ_TILES if r % t == 0)
    nsub = tm // TM
    x_ops = _stream_operands(stream, nsub)[0]
    x_specs = _stream_specs(split, nsub, ctx_len // TM)
    row = lambda bi, i: (bi, i, 0)
    widths = (COL_RW[1] - COL_RW[0], AT_DIM, 4 * LANE, COL_CV[1] - COL_CV[0])
    return pl.pallas_call(
        functools.partial(_proj_kernel, ctx_len=ctx_len, nx=len(x_specs)),
        grid=(b, r // tm),
        in_specs=x_specs + [
            pl.BlockSpec((None, 1, 3 * D_MODEL), lambda bi, i: (bi, 0, 0)),
            pl.BlockSpec((None, 1, 3 * D_MODEL), lambda bi, i: (b, 0, 0)),
            pl.BlockSpec((1, D_MODEL), lambda bi, i: (0, 0)),
            pl.BlockSpec((None, D_MODEL, D_IN), lambda bi, i: (layer, 0, 0), pipeline_mode=pl.Buffered(1)),
            pl.BlockSpec((tm, LANE), lambda bi, i: (i, 0)),
            pl.BlockSpec((tm, LANE), lambda bi, i: (i, 0)),
        ],
        out_specs=[pl.BlockSpec((None, tm, w), row) for w in widths],
        out_shape=[jax.ShapeDtypeStruct((b, r, w), BF16) for w in widths],
        compiler_params=pltpu.CompilerParams(
            dimension_semantics=("parallel", "parallel"), vmem_limit_bytes=VMEM_LIMIT),
        name="proj",
    )(*x_ops, mod, mod, norm_g, w_in, cos_t, sin_t)


_RW_FIELDS = ("at", "rt", "kt", "bt", "kh", "bh", "v")


def _group_masks(rev_lo, rev_hi):
    w = GROUP_W
    row = lax.broadcasted_iota(jnp.int32, (CHUNK, w), 0)
    lane = lax.broadcasted_iota(jnp.int32, (CHUNK, w), 1)
    s = lane % CHUNK
    x = row ^ s
    if rev_lo == rev_hi:
        before = (row - s) if rev_lo else (s - row)
    else:
        sign = jnp.where(lane >= LANE, -1 if rev_hi else 1, -1 if rev_lo else 1)
        before = (s - row) * sign
    m = {
        "strict": before < 0,
        "incl": before <= 0,
        "eye": s == row,
        "lvl1": (x >> 1) == 0,
        "heads": [(lane // HEAD_DIM) == h for h in range(w // HEAD_DIM)],
    }
    for lg in range(1, 6):
        m["lvl%d" % (1 << lg)] = (x >> lg) == 1
    return m


def _bd(x, m):
    return jnp.concatenate([jnp.where(hm, x, 0.0) for hm in m["heads"]], axis=0)


def _head_transpose(x):
    xt = x.T
    return jnp.concatenate([xt[h * HEAD_DIM:(h + 1) * HEAD_DIM] for h in range(x.shape[1] // HEAD_DIM)], axis=1)


def _chunk_pieces(ch, side=()):
    def advance():
        for gen in side:
            next(gen, None)

    n = range(len(ch))
    ms = [c["m"] for c in ch]
    advance()
    ar = [jnp.concatenate([c["at"], c["rt"]], axis=0) for c in ch]
    gb = [_mm(ar[i], _bd(_head_transpose(ch[i]["bt"]), ms[i])) for i in n]
    gk = [_mm(ar[i], _bd(_head_transpose(ch[i]["kt"]), ms[i])) for i in n]
    lo = [jnp.where(ms[i]["strict"], gb[i][:CHUNK], 0.0) for i in n]
    aak = [jnp.where(ms[i]["strict"], gk[i][:CHUNK], 0.0) for i in n]
    arb = [jnp.where(ms[i]["incl"], gb[i][CHUNK:], 0.0) for i in n]
    ark = [jnp.where(ms[i]["incl"], gk[i][CHUNK:], 0.0) for i in n]

    t = [jnp.where(ms[i]["eye"], 1.0, 0.0) - jnp.where(ms[i]["lvl1"], lo[i], 0.0) for i in n]
    for lvl in (2, 4, 8, 16, 32):
        x = [_mm(t[i], _bd(jnp.where(ms[i]["lvl%d" % lvl], lo[i], 0.0), ms[i])) for i in n]
        t = [t[i] - _mm(x[i], _bd(t[i], ms[i])) for i in n]
        advance()

    c2 = 2 * CHUNK
    bht = [_head_transpose(c["bh"]) for c in ch]
    kht = [_head_transpose(c["kh"]) for c in ch]
    av = [_mm(jnp.concatenate([aak[i], ark[i], kht[i]], axis=0), _bd(ch[i]["v"], ms[i])) for i in n]
    z = [_mm(jnp.concatenate([arb[i], bht[i]], axis=0), _bd(t[i], ms[i])) for i in n]
    pw = [_mm(z[i], _bd(ch[i]["at"], ms[i])) for i in n]
    pu = [_mm(z[i], _bd(av[i][:CHUNK], ms[i])) for i in n]
    qh = [ch[i]["rt"] - pw[i][:CHUNK] for i in n]
    yi = [av[i][CHUNK:c2] - pu[i][:CHUNK] for i in n]
    mm = [jnp.where(ms[i]["eye"], ch[i]["pe"], 0.0) - pw[i][CHUNK:] for i in n]
    nn = [av[i][c2:] - pu[i][CHUNK:] for i in n]
    return qh, yi, mm, nn


_RW_SLOT_FIELDS = _RW_FIELDS + ("pe",)
_RW_LAYOUT = (((0, 0, 256),), ((0, 256, 384), (1, 0, 128)), ((1, 128, 384),))


def _rwkv_prep(x_ref, d, w0_ref, wup_ref, a0_ref, aup_ref, kk_ref, ka_ref, dst):
    nchunk = TM // CHUNK
    rev = d == 1
    rt_i = lax.broadcasted_iota(jnp.int32, (TM, TM), 0)
    ct_i = lax.broadcasted_iota(jnp.int32, (TM, TM), 1)
    same_chunk = (rt_i // CHUNK) == (ct_i // CHUNK)
    lane = lax.broadcasted_iota(jnp.int32, (TM, LANE), 1)
    r = x_ref[:, 0:384].astype(F32)
    k = x_ref[:, 384:768].astype(F32)
    dmask = (lane // HEAD_DIM) == d
    tw = jnp.where(dmask, jnp.tanh(x_ref[:, 1536:1664].astype(F32)), 0.0)
    ta = jnp.where(dmask, x_ref[:, 1664:1792].astype(F32), 0.0)
    lw = w0_ref[d:d + 1, :] + _mm(tw, wup_ref[...])
    la = a0_ref[d:d + 1, :] + _mm(ta, aup_ref[...])
    kk = k * kk_ref[...]
    ss = _head_sum(kk * kk, passes=1)
    yield
    logw = -(DECAY_SCALE * LOG2E) * _sigmoid(lw)
    tri = jnp.where(same_chunk & ((ct_i >= rt_i) if rev else (ct_i <= rt_i)), 1.0, 0.0).astype(BF16)
    cl = _dot_split(tri, logw, left_exact=True)
    yield
    a = _sigmoid(la)
    kk = kk * lax.rsqrt(jnp.maximum(ss, 1e-12))
    kd = k * (1.0 + (a - 1.0) * ka_ref[...])
    beta = kk * a
    ends = [(c * CHUNK) if rev else (c * CHUNK + CHUNK - 1) for c in range(nchunk)]
    pe = jnp.concatenate([jnp.broadcast_to(jnp.exp2(cl[e:e + 1, :]), (CHUNK, RW_DIM)) for e in ends], axis=0)
    inv = jnp.exp2(-cl)
    kt = kd * inv
    bt = beta * inv
    dst["at"][...] = kk * jnp.exp2(cl - logw)
    dst["rt"][...] = r * jnp.exp2(cl)
    dst["kt"][...] = kt
    dst["bt"][...] = bt
    dst["kh"][...] = kt * pe
    dst["bh"][...] = bt * pe
    dst["v"][...] = x_ref[:, 768:1152].astype(F32)
    dst["pe"][...] = pe


def _rwkv_chains(src, masks, groups, row_off):
    nchunk = TM // CHUNK
    chains = []
    for step in range(nchunk):
        rows = (slice(step * CHUNK, (step + 1) * CHUNK),
                slice((nchunk - 1 - step) * CHUNK, (nchunk - step) * CHUNK))
        for g in groups:
            parts = _RW_LAYOUT[g]

            def gather(name):
                cols = [src[d][name][rows[d], lo:hi] for d, lo, hi in parts]
                return cols[0] if len(cols) == 1 else jnp.concatenate(cols, axis=1)
            ch = {name: gather(name) for name in _RW_FIELDS}
            ch.update(pe=gather("pe")[0:1, :], m=masks[g], g=g, step=step,
                      dst=[(d, slice(rows[d].start + row_off, rows[d].stop + row_off), lo, hi)
                           for d, lo, hi in parts])
            chains.append(ch)
    return chains


def _rwkv_state_pass(chains, pieces, yf_ref, yb_ref, h_refs):
    qh, yi, mm, nn = pieces
    y_refs = (yf_ref, yb_ref)
    h = [h_ref[...] for h_ref in h_refs]
    for step in range(TM // CHUNK):
        idx = [i for i, c in enumerate(chains) if c["step"] == step]
        res = {i: _mm(jnp.concatenate([qh[i], mm[i]], axis=0), _bd(h[chains[i]["g"]], chains[i]["m"]))
               for i in idx}
        for i in idx:
            ys = res[i][:CHUNK] + yi[i]
            off = 0
            for d, rws, lo, hi in chains[i]["dst"]:
                y_refs[d][rws, lo:hi] = ys[:, off:off + hi - lo].astype(yf_ref.dtype)
                off += hi - lo
            h[chains[i]["g"]] = res[i][CHUNK:] + nn[i]
        yield
    for h_ref, hv in zip(h_refs, h):
        h_ref[...] = hv


def _drain(*gens):
    for gen in gens:
        for _ in gen:
            pass


def _rwkv_kernel(*refs):
    x_refs, refs = refs[:2 * RW_TILES], refs[2 * RW_TILES:]
    params, (yf_ref, yb_ref), scratch = refs[:6], refs[6:8], refs[8:]
    ngroup = len(_RW_LAYOUT)
    h_refs = scratch[:ngroup]
    nf = len(_RW_SLOT_FIELDS)
    prepared = [[dict(zip(_RW_SLOT_FIELDS, scratch[ngroup + (t * 2 + d) * nf:ngroup + (t * 2 + d + 1) * nf]))
                 for d in range(2)] for t in range(RW_TILES)]

    @pl.when(pl.program_id(1) == 0)
    def _():
        for ref in h_refs:
            ref[...] = jnp.zeros_like(ref)

    masks = [_group_masks(False, False), _group_masks(False, True), _group_masks(True, True)]

    def prep(t, d):
        return _rwkv_prep(x_refs[d * RW_TILES + t], d, *params, prepared[t][d])

    _drain(prep(0, 0))
    pending = ()
    for t in range(RW_TILES):
        side = (prep(t, 1),) + pending
        chains_a = _rwkv_chains(prepared[t], masks, (0,), t * TM)
        pieces_a = _chunk_pieces(chains_a, side)
        _drain(*side)
        side = (prep(t + 1, 0),) if t + 1 < RW_TILES else ()
        chains_b = _rwkv_chains(prepared[t], masks, (1, 2), t * TM)
        pieces_b = _chunk_pieces(chains_b, side)
        _drain(*side)
        pieces = [a + b for a, b in zip(pieces_a, pieces_b)]
        pending = (_rwkv_state_pass(chains_a + chains_b, pieces, yf_ref, yb_ref, h_refs),)
    _drain(*pending)


def _bwd_tile(i, nct, nt):
    return jnp.where(i < nct, nct - 1 - i, nt - 1 - (i - nct))


def _rwkv(p_rw, w0, wup, a0, aup, k_k, k_a, nct):
    b, r, wcols = p_rw.shape
    nt = r // TM
    steps = -(-nt // RW_TILES)
    src = []
    for d in range(2):
        for t in range(RW_TILES):
            def index(bi, j, t=t, d=d):
                tile = jnp.minimum(j * RW_TILES + t, nt - 1)
                return (bi, _bwd_tile(tile, nct, nt) if d else tile, 0)
            src.append(pl.BlockSpec((None, TM, wcols), index))
    full = lambda shape: pl.BlockSpec(shape, lambda bi, j: (0,) * len(shape))
    dst = pl.BlockSpec((None, RW_TILES * TM, RW_DIM), lambda bi, j: (bi, j, 0))
    scratch = [pltpu.VMEM((HEAD_DIM, GROUP_W), F32) for _ in _RW_LAYOUT]
    scratch += [pltpu.VMEM((TM, RW_DIM), F32) for _ in range(RW_TILES * 2 * len(_RW_SLOT_FIELDS))]
    return pl.pallas_call(
        _rwkv_kernel,
        grid=(b, steps),
        in_specs=src + [
            full((2, RW_DIM)), full((2 * HEAD_DIM, RW_DIM)),
            full((2, RW_DIM)), full((2 * HEAD_DIM, RW_DIM)),
            full((1, RW_DIM)), full((1, RW_DIM)),
        ],
        out_specs=[dst, dst],
        out_shape=[jax.ShapeDtypeStruct((b, steps * RW_TILES * TM, RW_DIM), BF16)] * 2,
        scratch_shapes=scratch,
        compiler_params=pltpu.CompilerParams(
            dimension_semantics=("parallel", "arbitrary"), vmem_limit_bytes=VMEM_LIMIT),
        name="rwkv",
    )(*([p_rw] * (2 * RW_TILES)), w0, wup, a0, aup, k_k, k_a)


def _attn_compute(i, sink_ref, band_ref, q_ref, kp_ref, kc_ref, kn_ref, kx_ref, nct, nt):
    latent = i >= nct
    nloc = 3 * QB
    nkeys = nloc + kx_ref.shape[0]

    lane = lax.broadcasted_iota(jnp.int32, (QB, LANE), 1)
    low = lane < HEAD_DIM
    low_m = jnp.where(low, 1.0, 0.0).astype(BF16)
    high_m = jnp.where(low, 0.0, 1.0).astype(BF16)
    col = lax.broadcasted_iota(jnp.int32, (1, nkeys), 1)
    hrow = lax.broadcasted_iota(jnp.int32, (3 * QB, 1), 0) // QB
    band = band_ref[...]
    head_src = (((0, low_m), (0, high_m), (1, low_m)), ((1, high_m), (2, low_m), (2, high_m)))

    qs, kall, vall, bias, sk = [], [], [], [], []
    for qb in range(TM // QB):
        qrows = slice(qb * QB, (qb + 1) * QB)
        qt = [q_ref[qrows, LANE * t:LANE * (t + 1)] for t in range(AT_DIM // LANE)]
        halves = [(kc_ref, slice(0, QB)), (kc_ref, slice(QB, 2 * QB))]
        if qb == 0:
            blocks = [(kp_ref, slice(0, QB))] + halves
            first_ok, last_ok = i > nct, True
        else:
            blocks = halves + [(kn_ref, slice(0, QB))]
            first_ok, last_ok = True, i < nt - 1
        lo_c = jnp.where(jnp.logical_and(latent, first_ok), 0, jnp.where(latent, QB, nloc))
        hi_c = jnp.where(last_ok, nloc, 2 * QB)
        dead = jnp.logical_and(col < nloc, jnp.logical_or(col < lo_c, col >= hi_c))
        bias_qb = band + jnp.where(dead, MASK_VALUE, 0.0)
        for g in range(2):
            qs.append(jnp.concatenate([qt[t] * hm for t, hm in head_src[g]], axis=0))
            kcols = slice(LANE * g, LANE * (g + 1))
            vcols = slice(2 * LANE + LANE * g, 2 * LANE + LANE * (g + 1))
            kall.append(jnp.concatenate([ref[rs, kcols] for ref, rs in blocks] + [kx_ref[:, kcols]], axis=0))
            vall.append(jnp.concatenate([ref[rs, vcols] for ref, rs in blocks] + [kx_ref[:, vcols]], axis=0))
            bias.append(bias_qb)
            sk.append(LOG2E * jnp.where(hrow == 0, sink_ref[3 * g],
                                        jnp.where(hrow == 1, sink_ref[3 * g + 1], sink_ref[3 * g + 2])))

    n = range(len(qs))
    s = [_mm_nt(qs[c], kall[c]) + bias[c] for c in n]
    mx = [jnp.maximum(jnp.max(s[c], axis=-1, keepdims=True), sk[c]) for c in n]
    p = [jnp.exp2(s[c] - mx[c]) for c in n]
    den = [jnp.sum(p[c], axis=-1, keepdims=True) + jnp.exp2(sk[c] - mx[c]) for c in n]
    o = [_mm(p[c].astype(BF16), vall[c]) / den[c] for c in n]

    blocks = []
    for qb in range(TM // QB):
        o0, o1 = o[2 * qb], o[2 * qb + 1]
        blocks.append(jnp.concatenate([
            jnp.where(low, o0[0:QB], o0[QB:2 * QB]),
            jnp.where(low, o0[2 * QB:3 * QB], o1[0:QB]),
            jnp.where(low, o1[QB:2 * QB], o1[2 * QB:3 * QB]),
        ], axis=1))
    return jnp.concatenate(blocks, axis=0)


def _attn_band(ctx_len):
    t = jnp.arange(3 * QB)[:, None] % QB
    col = jnp.arange(3 * QB + ctx_len)[None, :]
    ok = ((col >= t) & (col <= t + 2 * WINDOW)) | (col >= 3 * QB)
    return jnp.where(ok, 0.0, MASK_VALUE).astype(F32)


def _attn_specs(r, off, ctx_len, band_shape, kvw):
    nct = ctx_len // TM
    per = TM // QB
    nb = r // QB
    clamp = lambda j: jnp.clip(j, nct * per, nb - 1)
    return [
        pl.BlockSpec(memory_space=pltpu.SMEM),
        pl.BlockSpec(band_shape, lambda bi, i: (0, 0)),
        pl.BlockSpec((None, TM, AT_DIM), lambda bi, i: (bi, i + off, 0)),
        pl.BlockSpec((None, QB, kvw), lambda bi, i: (bi, clamp((i + off) * per - 1), 0)),
        pl.BlockSpec((None, TM, kvw), lambda bi, i: (bi, i + off, 0)),
        pl.BlockSpec((None, QB, kvw), lambda bi, i: (bi, clamp((i + off + 1) * per), 0)),
        pl.BlockSpec((None, ctx_len, kvw), lambda bi, i: (bi, 0, 0)),
    ]


def _out_kernel(*refs, nx, tile_off, ctx_len, rows, final):
    x_refs, attn_refs, rest = refs[:nx], refs[nx:nx + 7], refs[nx + 7:]
    fg_ref, o_ref = rest[-2:]
    tile = pl.program_id(1) + tile_off
    x = _stream_tile(x_refs, tile, ctx_len // TM)
    partial, at_gate = _mix_compute(tile * TM, *rest[1:-2], ctx_len, rows)
    y_at = _attn_compute(tile, *attn_refs, ctx_len // TM, rows // TM)
    mod_ref, wout_ref = rest[0], rest[-3]
    acc = partial + _mm(y_at * at_gate, wout_ref[384:768, :])
    xn = x + mod_ref[:, 2 * D_MODEL:3 * D_MODEL] * acc
    if final:
        ms = jnp.mean(xn * xn, axis=-1, keepdims=True)
        xn = xn * lax.rsqrt(ms + NORM_EPS) * fg_ref[...]
    o_ref[...] = xn


def _mix_compute(row0, rw_ref, yf_ref, yb_ref, cv_ref, cvp_ref, cvn_ref,
                 rk_ref, lng_ref, lnb_ref, cvw_ref, wout_ref, ctx_len, rows):
    r = rw_ref[:, 0:384].astype(F32)
    k = rw_ref[:, 384:768].astype(F32)
    v = rw_ref[:, 768:1152].astype(F32)
    g = rw_ref[:, 1152:1536].astype(F32)

    y = yf_ref[...].astype(F32) + yb_ref[...].astype(F32)
    dev = y - _head_sum(y, 1.0 / HEAD_DIM)
    var = _head_sum(dev * dev, 1.0 / HEAD_DIM, passes=1)
    yn = dev * lax.rsqrt(var + LNX_EPS) * lng_ref[...] + lnb_ref[...]
    bonus = _head_sum(r * k * rk_ref[...], passes=1) * v
    y_rw = (yn + bonus) * (g * _sigmoid(g))

    g_at = cv_ref[:, 0:384].astype(F32)
    at_gate = g_at * _sigmoid(g_at)

    bg = cv_ref[:, 384:640].astype(F32)
    u = cv_ref[:, 640:896].astype(F32) * cv_ref[:, 896:1152].astype(F32)
    g_cv = cv_ref[:, 1152:1408].astype(F32)
    prev_ok = jnp.logical_and(row0 != 0, row0 != ctx_len)
    next_ok = jnp.logical_and(row0 + TM != ctx_len, row0 + TM != rows)
    pu = cvp_ref[:, 640:896].astype(F32) * cvp_ref[:, 896:1152].astype(F32)
    nu = cvn_ref[:, 640:896].astype(F32) * cvn_ref[:, 896:1152].astype(F32)
    u_prev = jnp.where(prev_ok, pu[HALO - 1:HALO, :], 0.0)
    u_next = jnp.where(next_ok, nu[0:1, :], 0.0)
    ridx = lax.broadcasted_iota(jnp.int32, (TM, CV_DIM), 0)
    um = jnp.where(ridx == 0, u_prev, pltpu.roll(u, 1, 0))
    up = jnp.where(ridx == TM - 1, u_next, pltpu.roll(u, TM - 1, 0))
    conv = um * cvw_ref[0:1, :] + u * cvw_ref[1:2, :] + up * cvw_ref[2:3, :]
    y_cv = bg * conv * (g_cv * _sigmoid(g_cv))

    return _mm(y_rw, wout_ref[0:384, :]) + _mm(y_cv, wout_ref[768:1024, :]), at_gate


def _out_in_specs(b, r, off, nct, cvw, layer):
    hb = TM // HALO
    row = lambda bi, i: (bi, i + off, 0)
    full = lambda shape: pl.BlockSpec(shape, lambda bi, i: (0,) * len(shape))
    return [
        pl.BlockSpec((None, 1, 3 * D_MODEL), lambda bi, i: (jnp.where(i + off < nct, b, bi), 0, 0)),
        pl.BlockSpec((None, TM, 1536), row),
        pl.BlockSpec((None, TM, RW_DIM), row),
        pl.BlockSpec((None, TM, RW_DIM), lambda bi, i: (bi, _bwd_tile(i + off, nct, r // TM), 0)),
        pl.BlockSpec((None, TM, cvw), row),
        pl.BlockSpec((None, HALO, cvw), lambda bi, i: (bi, jnp.maximum((i + off) * hb - 1, 0), 0)),
        pl.BlockSpec((None, HALO, cvw), lambda bi, i: (bi, jnp.minimum((i + off + 1) * hb, r // HALO - 1), 0)),
        full((1, RW_DIM)), full((1, RW_DIM)), full((1, RW_DIM)),
        full((3, CV_DIM)),
        pl.BlockSpec((None, D_MODEL, D_MODEL), lambda bi, i: (layer, 0, 0)),
    ]


def _out(stream, p_q, p_kv, sink, band, mod, p_rw, y_f, y_b, p_cv, r_k, ln_g, ln_b, cv_w, w_out, layer, final_g,
         nct, ctx_len, final):
    x_ops, split, b, r = _stream_operands(stream, 1)
    nt = r // TM
    off = nct if final else 0
    x_specs = _stream_specs(split, 1, nct, off)
    return pl.pallas_call(
        functools.partial(_out_kernel, nx=len(x_specs), tile_off=off, ctx_len=ctx_len, rows=r, final=final),
        grid=(b, nt - off),
        in_specs=x_specs + _attn_specs(r, off, ctx_len, band.shape, p_kv.shape[-1])
        + _out_in_specs(b, r, off, nct, p_cv.shape[-1], layer)
        + [pl.BlockSpec((1, D_MODEL), lambda bi, i: (0, 0))],
        out_specs=pl.BlockSpec((None, TM, D_MODEL), lambda bi, i: (bi, i, 0)),
        out_shape=jax.ShapeDtypeStruct((b, r - off * TM, D_MODEL), F32),
        compiler_params=pltpu.CompilerParams(
            dimension_semantics=("parallel", "parallel"), vmem_limit_bytes=VMEM_LIMIT),
        name="out",
    )(*x_ops, sink, band, p_q, p_kv, p_kv, p_kv, p_kv,
      mod, p_rw, y_f, y_b, p_cv, p_cv, p_cv, r_k, ln_g, ln_b, cv_w, w_out, final_g)


def _rope_tables(seq, ctx_len):
    t = jnp.arange(seq)
    nfreq = HEAD_DIM // 4
    inv = ROPE_THETA ** (-jnp.arange(nfreq, dtype=F32) / nfreq)
    ang = jnp.concatenate([(t // GRID_W).astype(F32)[:, None] * inv,
                           (t % GRID_W).astype(F32)[:, None] * inv], axis=-1)
    cos, sin = jnp.cos(ang), jnp.sin(ang)
    reps = LANE // HEAD_DIM
    cos_t = jnp.tile(jnp.concatenate([cos, cos], axis=-1), (1, reps))
    sin_t = jnp.tile(jnp.concatenate([-sin, sin], axis=-1), (1, reps))
    cos_t = jnp.concatenate([jnp.ones((ctx_len, LANE), F32), cos_t], axis=0)
    sin_t = jnp.concatenate([jnp.zeros((ctx_len, LANE), F32), sin_t], axis=0)
    return cos_t, sin_t


def kernel(x, c, ctx, c_ctx, w_ada, b_ada, norm_g, w_in, w_out, rw_w0, rw_w_up, rw_a0, rw_a_up,
           rw_k_k, rw_k_a, rw_r_k, rw_ln_g, rw_ln_b, at_sink, cv_w, final_g):
    b, seq, _ = x.shape
    ctx_len = ctx.shape[1]
    depth = w_in.shape[0]
    assert ctx_len % TM == 0 and seq % TM == 0 and b + 1 <= 8
    nct = ctx_len // TM

    xa = (x, ctx)
    cvec = jnp.zeros((8, D_MODEL), F32).at[0:b].set(c).at[b].set(c_ctx)
    mod = _ada(cvec, w_ada, b_ada)
    cos_t, sin_t = _rope_tables(seq, ctx_len)
    band = _attn_band(ctx_len)

    for l in range(depth):
        mod_l = mod[l].reshape(8, 1, 3 * D_MODEL)
        p_rw, p_q, p_kv, p_cv = _proj(xa, mod_l, norm_g[l].reshape(1, -1), w_in, l, cos_t, sin_t, ctx_len)
        y_f, y_b = _rwkv(p_rw, rw_w0[l], rw_w_up[l].reshape(2 * HEAD_DIM, RW_DIM), rw_a0[l],
                         rw_a_up[l].reshape(2 * HEAD_DIM, RW_DIM), rw_k_k[l].reshape(1, -1),
                         rw_k_a[l].reshape(1, -1), nct)
        xa = _out(xa, p_q, p_kv, at_sink[l], band, mod_l, p_rw, y_f, y_b, p_cv, rw_r_k[l].reshape(1, -1),
                  rw_ln_g[l].reshape(1, -1), rw_ln_b[l].reshape(1, -1), cv_w[l], w_out, l,
                  final_g.reshape(1, -1), nct, ctx_len, final=(l == depth - 1))
    return xa
```

```python
import functools

import jax
import jax.numpy as jnp
from jax import lax
from jax.experimental import pallas as pl
from jax.experimental.pallas import tpu as pltpu

F32 = jnp.float32
BF16 = jnp.bfloat16

D_MODEL = 1024
HEAD_DIM = 64
GRID_W = 64
ROPE_THETA = 10000.0
RW_DIM = 384
AT_DIM = 384
CV_DIM = 256
WINDOW = 128
NORM_EPS = 1e-6
LNX_EPS = 64e-5
MASK_VALUE = -1e30
DECAY_SCALE = 0.6065306597126334
LOG2E = 1.4426950408889634
Q_SCALE = HEAD_DIM ** -0.5 * LOG2E

COL_RW = (0, 1792)
COL_Q = (1792, 2176)
COL_KV = (2176, 2432)
COL_CV = (2432, 3840)
D_IN = 3840

TM = 256
PROJ_TILES = (768, 512, 256)
CHUNK = 64
RW_TILES = 3
QB = 128
LANE = 128
HALO = 16
GROUP_W = 256
VMEM_LIMIT = 56 * 1024 * 1024


def _mm(a, b):
    return jnp.dot(a, b, preferred_element_type=F32)


def _mm_nt(a, b):
    return lax.dot_general(a, b, (((1,), (1,)), ((), ())), preferred_element_type=F32)


def _dot_split(a, b, passes=2, left_exact=False):
    acc = None
    rem = a if not left_exact else b
    for p in range(passes):
        hi = rem.astype(BF16)
        t = jnp.dot(a, hi, preferred_element_type=F32) if left_exact else jnp.dot(hi, b, preferred_element_type=F32)
        acc = t if acc is None else acc + t
        if p + 1 < passes:
            rem = rem - hi.astype(F32)
    return acc


def _sigmoid(x):
    return 1.0 / (1.0 + jnp.exp(-x))


def _head_ones(n, scale=1.0):
    r = lax.broadcasted_iota(jnp.int32, (n, n), 0) // HEAD_DIM
    c = lax.broadcasted_iota(jnp.int32, (n, n), 1) // HEAD_DIM
    return jnp.where(r == c, scale, 0.0).astype(BF16)


def _head_sum(x, scale=1.0, passes=2):
    parts = []
    for lo in range(0, x.shape[1], GROUP_W):
        hi = min(lo + GROUP_W, x.shape[1])
        parts.append(_dot_split(x[:, lo:hi], _head_ones(hi - lo, scale), passes))
    return jnp.concatenate(parts, axis=1)


def _ada_kernel(c_ref, w_ref, b_ref, o_ref):
    c = c_ref[...]
    a = c * _sigmoid(c)
    w = w_ref[...]
    a_hi = a.astype(BF16)
    w_hi = w.astype(BF16)
    a_lo = (a - a_hi.astype(F32)).astype(BF16)
    w_lo = (w - w_hi.astype(F32)).astype(BF16)
    o_ref[...] = _mm(a_hi, w_hi) + _mm(a_lo, w_hi) + _mm(a_hi, w_lo) + b_ref[...]


def _ada(cvec, w_ada, b_ada):
    depth = w_ada.shape[0]
    tn = 1024
    return pl.pallas_call(
        _ada_kernel,
        grid=(depth, 3 * D_MODEL // tn),
        in_specs=[
            pl.BlockSpec((8, D_MODEL), lambda l, n: (0, 0)),
            pl.BlockSpec((None, D_MODEL, tn), lambda l, n: (l, 0, n)),
            pl.BlockSpec((None, 1, tn), lambda l, n: (l, 0, n)),
        ],
        out_specs=pl.BlockSpec((None, 8, tn), lambda l, n: (l, 0, n)),
        out_shape=jax.ShapeDtypeStruct((depth, 8, 3 * D_MODEL), F32),
        compiler_params=pltpu.CompilerParams(vmem_limit_bytes=VMEM_LIMIT),
        name="ada",
    )(cvec, w_ada, b_ada.reshape(depth, 1, 3 * D_MODEL))


def _stream_tile(x_refs, tile0, nct, nsub):
    if len(x_refs) == 1:
        return x_refs[0][...]
    blocks = []
    for k in range(nsub):
        blk = x_refs[k][...]
        if nsub + k < len(x_refs):
            blk = jnp.where(tile0 + k < nct, x_refs[nsub + k][...], blk)
        blocks.append(blk)
    return blocks[0] if nsub == 1 else jnp.concatenate(blocks, axis=0)


def _stream_specs(split, nsub, nct, off=0):
    if not split:
        return [pl.BlockSpec((None, nsub * TM, D_MODEL), lambda bi, i: (bi, i + off, 0))]
    lat = [pl.BlockSpec((None, TM, D_MODEL),
                        lambda bi, i, k=k: (bi, jnp.maximum((i + off) * nsub + k - nct, 0), 0)) for k in range(nsub)]
    ctx = [pl.BlockSpec((None, TM, D_MODEL),
                        lambda bi, i, k=k: (bi, jnp.minimum((i + off) * nsub + k, nct - 1), 0))
           for k in range(min(nsub, nct))]
    return lat + ctx


def _proj_kernel(*refs, ctx_len, nx):
    x_refs, (modb_ref, modc_ref, g_ref, w_ref, cos_ref, sin_ref, rw_ref, q_ref, kv_ref, cv_ref) = refs[:nx], refs[nx:]
    tm = rw_ref.shape[0]
    i = pl.program_id(1)
    x = _stream_tile(x_refs, i * (tm // TM), ctx_len // TM, tm // TM)
    _proj_compute(x, i * tm, modb_ref, modc_ref, g_ref, w_ref, cos_ref, sin_ref,
                  rw_ref, q_ref, kv_ref, cv_ref, ctx_len)


def _proj_compute(x, row0, modb_ref, modc_ref, g_ref, w_ref, cos_ref, sin_ref, rw_ref, q_ref, kv_ref, cv_ref,
                  ctx_len):
    tm = x.shape[0]
    ms = jnp.mean(x * x, axis=-1, keepdims=True)
    y = x * lax.rsqrt(ms + NORM_EPS) * g_ref[...]
    mb = modb_ref[...]
    mc = modc_ref[...]
    blocks = []
    for k in range(tm // TM):
        is_ctx = row0 + k * TM < ctx_len
        shift = jnp.where(is_ctx, mc[:, 0:D_MODEL], mb[:, 0:D_MODEL])
        gain = 1.0 + jnp.where(is_ctx, mc[:, D_MODEL:2 * D_MODEL], mb[:, D_MODEL:2 * D_MODEL])
        blocks.append(y[k * TM:(k + 1) * TM] * gain + shift)
    h = blocks[0] if len(blocks) == 1 else jnp.concatenate(blocks, axis=0)

    def seg(cols):
        return jnp.dot(h, w_ref[:, cols[0]:cols[1]], preferred_element_type=F32)

    cos = cos_ref[...]
    sin = sin_ref[...]
    lane = lax.broadcasted_iota(jnp.int32, (tm, LANE), 1)
    first_half = (lane % HEAD_DIM) < (HEAD_DIM // 2)
    low_head = lane < HEAD_DIM

    def rope(t):
        rot = jnp.where(first_half, pltpu.roll(t, LANE - 32, 1), pltpu.roll(t, 32, 1))
        return t * cos + rot * sin

    def dup(t):
        sw = pltpu.roll(t, HEAD_DIM, 1)
        return jnp.where(low_head, t, sw), jnp.where(low_head, sw, t)

    q = seg(COL_Q)
    q_ref[...] = jnp.concatenate(
        [rope(q[:, LANE * j:LANE * (j + 1)]) * Q_SCALE for j in range(AT_DIM // LANE)],
        axis=1).astype(BF16)
    kv = seg(COL_KV)
    k0, k1 = dup(rope(kv[:, 0:LANE]))
    v0, v1 = dup(kv[:, LANE:2 * LANE])
    kv_ref[...] = jnp.concatenate([k0, k1, v0, v1], axis=1).astype(BF16)
    rw_ref[...] = seg(COL_RW).astype(BF16)
    cv_ref[...] = seg(COL_CV).astype(BF16)


def _stream_operands(stream, nsub):
    if isinstance(stream, tuple):
        x, ctx = stream
        return [x] * nsub + [ctx] * min(nsub, ctx.shape[1] // TM), True, x.shape[0], x.shape[1] + ctx.shape[1]
    return [stream], False, stream.shape[0], stream.shape[1]


def _proj(stream, mod, norm_g, w_in, layer, cos_t, sin_t, ctx_len):
    _, split, b, r = _stream_operands(stream, 1)
    tm = next(t for t in PROJ_TILES if r % t == 0)
    nsub = tm // TM
    x_ops = _stream_operands(stream, nsub)[0]
    x_specs = _stream_specs(split, nsub, ctx_len // TM)
    row = lambda bi, i: (bi, i, 0)
    widths = (COL_RW[1] - COL_RW[0], AT_DIM, 4 * LANE, COL_CV[1] - COL_CV[0])
    return pl.pallas_call(
        functools.partial(_proj_kernel, ctx_len=ctx_len, nx=len(x_specs)),
        grid=(b, r // tm),
        in_specs=x_specs + [
            pl.BlockSpec((None, 1, 3 * D_MODEL), lambda bi, i: (bi, 0, 0)),
            pl.BlockSpec((None, 1, 3 * D_MODEL), lambda bi, i: (b, 0, 0)),
            pl.BlockSpec((1, D_MODEL), lambda bi, i: (0, 0)),
            pl.BlockSpec((None, D_MODEL, D_IN), lambda bi, i: (layer, 0, 0), pipeline_mode=pl.Buffered(1)),
            pl.BlockSpec((tm, LANE), lambda bi, i: (i, 0)),
            pl.BlockSpec((tm, LANE), lambda bi, i: (i, 0)),
        ],
        out_specs=[pl.BlockSpec((None, tm, w), row) for w in widths],
        out_shape=[jax.ShapeDtypeStruct((b, r, w), BF16) for w in widths],
        compiler_params=pltpu.CompilerParams(
            dimension_semantics=("parallel", "parallel"), vmem_limit_bytes=VMEM_LIMIT),
        name="proj",
    )(*x_ops, mod, mod, norm_g, w_in, cos_t, sin_t)


_RW_FIELDS = ("at", "rt", "kt", "bt", "kh", "bh", "v")


def _group_masks(rev_lo, rev_hi):
    w = GROUP_W
    row = lax.broadcasted_iota(jnp.int32, (CHUNK, w), 0)
    lane = lax.broadcasted_iota(jnp.int32, (CHUNK, w), 1)
    s = lane % CHUNK
    x = row ^ s
    if rev_lo == rev_hi:
        before = (row - s) if rev_lo else (s - row)
    else:
        sign = jnp.where(lane >= LANE, -1 if rev_hi else 1, -1 if rev_lo else 1)
        before = (s - row) * sign
    m = {
        "strict": before < 0,
        "incl": before <= 0,
        "eye": s == row,
        "lvl1": (x >> 1) == 0,
        "heads": [(lane // HEAD_DIM) == h for h in range(w // HEAD_DIM)],
    }
    for lg in range(1, 6):
        m["lvl%d" % (1 << lg)] = (x >> lg) == 1
    return m


def _bd(x, m):
    return jnp.concatenate([jnp.where(hm, x, 0.0) for hm in m["heads"]], axis=0)


def _head_transpose(x):
    xt = x.T
    return jnp.concatenate([xt[h * HEAD_DIM:(h + 1) * HEAD_DIM] for h in range(x.shape[1] // HEAD_DIM)], axis=1)


def _chunk_pieces(ch, side=()):
    def advance():
        for gen in side:
            next(gen, None)

    n = range(len(ch))
    ms = [c["m"] for c in ch]
    advance()
    ar = [jnp.concatenate([c["at"], c["rt"]], axis=0) for c in ch]
    gb = [_mm(ar[i], _bd(_head_transpose(ch[i]["bt"]), ms[i])) for i in n]
    gk = [_mm(ar[i], _bd(_head_transpose(ch[i]["kt"]), ms[i])) for i in n]
    lo = [jnp.where(ms[i]["strict"], gb[i][:CHUNK], 0.0) for i in n]
    aak = [jnp.where(ms[i]["strict"], gk[i][:CHUNK], 0.0) for i in n]
    arb = [jnp.where(ms[i]["incl"], gb[i][CHUNK:], 0.0) for i in n]
    ark = [jnp.where(ms[i]["incl"], gk[i][CHUNK:], 0.0) for i in n]

    t = [jnp.where(ms[i]["eye"], 1.0, 0.0) - jnp.where(ms[i]["lvl1"], lo[i], 0.0) for i in n]
    for lvl in (2, 4, 8, 16, 32):
        x = [_mm(t[i], _bd(jnp.where(ms[i]["lvl%d" % lvl], lo[i], 0.0), ms[i])) for i in n]
        t = [t[i] - _mm(x[i], _bd(t[i], ms[i])) for i in n]
        advance()

    c2 = 2 * CHUNK
    bht = [_head_transpose(c["bh"]) for c in ch]
    kht = [_head_transpose(c["kh"]) for c in ch]
    av = [_mm(jnp.concatenate([aak[i], ark[i], kht[i]], axis=0), _bd(ch[i]["v"], ms[i])) for i in n]
    z = [_mm(jnp.concatenate([arb[i], bht[i]], axis=0), _bd(t[i], ms[i])) for i in n]
    pw = [_mm(z[i], _bd(ch[i]["at"], ms[i])) for i in n]
    pu = [_mm(z[i], _bd(av[i][:CHUNK], ms[i])) for i in n]
    qh = [ch[i]["rt"] - pw[i][:CHUNK] for i in n]
    yi = [av[i][CHUNK:c2] - pu[i][:CHUNK] for i in n]
    mm = [jnp.where(ms[i]["eye"], ch[i]["pe"], 0.0) - pw[i][CHUNK:] for i in n]
    nn = [av[i][c2:] - pu[i][CHUNK:] for i in n]
    return qh, yi, mm, nn


_RW_SLOT_FIELDS = _RW_FIELDS + ("pe",)
_RW_LAYOUT = (((0, 0, 256),), ((0, 256, 384), (1, 0, 128)), ((1, 128, 384),))


def _rwkv_prep(x_ref, d, w0_ref, wup_ref, a0_ref, aup_ref, kk_ref, ka_ref, dst):
    nchunk = TM // CHUNK
    rev = d == 1
    rt_i = lax.broadcasted_iota(jnp.int32, (TM, TM), 0)
    ct_i = lax.broadcasted_iota(jnp.int32, (TM, TM), 1)
    same_chunk = (rt_i // CHUNK) == (ct_i // CHUNK)
    lane = lax.broadcasted_iota(jnp.int32, (TM, LANE), 1)
    r = x_ref[:, 0:384].astype(F32)
    k = x_ref[:, 384:768].astype(F32)
    dmask = (lane // HEAD_DIM) == d
    tw = jnp.where(dmask, jnp.tanh(x_ref[:, 1536:1664].astype(F32)), 0.0)
    ta = jnp.where(dmask, x_ref[:, 1664:1792].astype(F32), 0.0)
    lw = w0_ref[d:d + 1, :] + _mm(tw, wup_ref[...])
    la = a0_ref[d:d + 1, :] + _mm(ta, aup_ref[...])
    kk = k * kk_ref[...]
    ss = _head_sum(kk * kk, passes=1)
    yield
    logw = -(DECAY_SCALE * LOG2E) * _sigmoid(lw)
    tri = jnp.where(same_chunk & ((ct_i >= rt_i) if rev else (ct_i <= rt_i)), 1.0, 0.0).astype(BF16)
    cl = _dot_split(tri, logw, left_exact=True)
    yield
    a = _sigmoid(la)
    kk = kk * lax.rsqrt(jnp.maximum(ss, 1e-12))
    kd = k * (1.0 + (a - 1.0) * ka_ref[...])
    beta = kk * a
    ends = [(c * CHUNK) if rev else (c * CHUNK + CHUNK - 1) for c in range(nchunk)]
    pe = jnp.concatenate([jnp.broadcast_to(jnp.exp2(cl[e:e + 1, :]), (CHUNK, RW_DIM)) for e in ends], axis=0)
    inv = jnp.exp2(-cl)
    kt = kd * inv
    bt = beta * inv
    dst["at"][...] = kk * jnp.exp2(cl - logw)
    dst["rt"][...] = r * jnp.exp2(cl)
    dst["kt"][...] = kt
    dst["bt"][...] = bt
    dst["kh"][...] = kt * pe
    dst["bh"][...] = bt * pe
    dst["v"][...] = x_ref[:, 768:1152].astype(F32)
    dst["pe"][...] = pe


def _rwkv_chains(src, masks, groups, row_off):
    nchunk = TM // CHUNK
    chains = []
    for step in range(nchunk):
        rows = (slice(step * CHUNK, (step + 1) * CHUNK),
                slice((nchunk - 1 - step) * CHUNK, (nchunk - step) * CHUNK))
        for g in groups:
            parts = _RW_LAYOUT[g]

            def gather(name):
                cols = [src[d][name][rows[d], lo:hi] for d, lo, hi in parts]
                return cols[0] if len(cols) == 1 else jnp.concatenate(cols, axis=1)
            ch = {name: gather(name) for name in _RW_FIELDS}
            ch.update(pe=gather("pe")[0:1, :], m=masks[g], g=g, step=step,
                      dst=[(d, slice(rows[d].start + row_off, rows[d].stop + row_off), lo, hi)
                           for d, lo, hi in parts])
            chains.append(ch)
    return chains


def _rwkv_state_pass(chains, pieces, yf_ref, yb_ref, h_refs):
    qh, yi, mm, nn = pieces
    y_refs = (yf_ref, yb_ref)
    h = [h_ref[...] for h_ref in h_refs]
    for step in range(TM // CHUNK):
        idx = [i for i, c in enumerate(chains) if c["step"] == step]
        res = {i: _mm(jnp.concatenate([qh[i], mm[i]], axis=0), _bd(h[chains[i]["g"]], chains[i]["m"]))
               for i in idx}
        for i in idx:
            ys = res[i][:CHUNK] + yi[i]
            off = 0
            for d, rws, lo, hi in chains[i]["dst"]:
                y_refs[d][rws, lo:hi] = ys[:, off:off + hi - lo].astype(yf_ref.dtype)
                off += hi - lo
            h[chains[i]["g"]] = res[i][CHUNK:] + nn[i]
        yield
    for h_ref, hv in zip(h_refs, h):
        h_ref[...] = hv


def _drain(*gens):
    for gen in gens:
        for _ in gen:
            pass


def _rwkv_kernel(*refs):
    x_refs, refs = refs[:2 * RW_TILES], refs[2 * RW_TILES:]
    params, (yf_ref, yb_ref), scratch = refs[:6], refs[6:8], refs[8:]
    ngroup = len(_RW_LAYOUT)
    h_refs = scratch[:ngroup]
    nf = len(_RW_SLOT_FIELDS)
    prepared = [[dict(zip(_RW_SLOT_FIELDS, scratch[ngroup + (t * 2 + d) * nf:ngroup + (t * 2 + d + 1) * nf]))
                 for d in range(2)] for t in range(RW_TILES)]

    @pl.when(pl.program_id(1) == 0)
    def _():
        for ref in h_refs:
            ref[...] = jnp.zeros_like(ref)

    masks = [_group_masks(False, False), _group_masks(False, True), _group_masks(True, True)]

    def prep(t, d):
        return _rwkv_prep(x_refs[d * RW_TILES + t], d, *params, prepared[t][d])

    _drain(prep(0, 0))
    pending = ()
    for t in range(RW_TILES):
        side = (prep(t, 1),) + pending
        chains_a = _rwkv_chains(prepared[t], masks, (0,), t * TM)
        pieces_a = _chunk_pieces(chains_a, side)
        _drain(*side)
        side = (prep(t + 1, 0),) if t + 1 < RW_TILES else ()
        chains_b = _rwkv_chains(prepared[t], masks, (1, 2), t * TM)
        pieces_b = _chunk_pieces(chains_b, side)
        _drain(*side)
        pieces = [a + b for a, b in zip(pieces_a, pieces_b)]
        pending = (_rwkv_state_pass(chains_a + chains_b, pieces, yf_ref, yb_ref, h_refs),)
    _drain(*pending)


def _bwd_tile(i, nct, nt):
    return jnp.where(i < nct, nct - 1 - i, nt - 1 - (i - nct))


def _rwkv(p_rw, w0, wup, a0, aup, k_k, k_a, nct):
    b, r, wcols = p_rw.shape
    nt = r // TM
    steps = -(-nt // RW_TILES)
    src = []
    for d in range(2):
        for t in range(RW_TILES):
            def index(bi, j, t=t, d=d):
                tile = jnp.minimum(j * RW_TILES + t, nt - 1)
                return (bi, _bwd_tile(tile, nct, nt) if d else tile, 0)
            src.append(pl.BlockSpec((None, TM, wcols), index))
    full = lambda shape: pl.BlockSpec(shape, lambda bi, j: (0,) * len(shape))
    dst = pl.BlockSpec((None, RW_TILES * TM, RW_DIM), lambda bi, j: (bi, j, 0))
    scratch = [pltpu.VMEM((HEAD_DIM, GROUP_W), F32) for _ in _RW_LAYOUT]
    scratch += [pltpu.VMEM((TM, RW_DIM), F32) for _ in range(RW_TILES * 2 * len(_RW_SLOT_FIELDS))]
    return pl.pallas_call(
        _rwkv_kernel,
        grid=(b, steps),
        in_specs=src + [
            full((2, RW_DIM)), full((2 * HEAD_DIM, RW_DIM)),
            full((2, RW_DIM)), full((2 * HEAD_DIM, RW_DIM)),
            full((1, RW_DIM)), full((1, RW_DIM)),
        ],
        out_specs=[dst, dst],
        out_shape=[jax.ShapeDtypeStruct((b, steps * RW_TILES * TM, RW_DIM), BF16)] * 2,
        scratch_shapes=scratch,
        compiler_params=pltpu.CompilerParams(
            dimension_semantics=("parallel", "arbitrary"), vmem_limit_bytes=VMEM_LIMIT),
        name="rwkv",
    )(*([p_rw] * (2 * RW_TILES)), w0, wup, a0, aup, k_k, k_a)


def _attn_compute(i, sink_ref, band_ref, q_ref, kp_ref, kc_ref, kn_ref, kx_ref, nct, nt):
    latent = i >= nct
    nloc = 3 * QB
    nkeys = nloc + kx_ref.shape[0]

    lane = lax.broadcasted_iota(jnp.int32, (QB, LANE), 1)
    low = lane < HEAD_DIM
    low_m = jnp.where(low, 1.0, 0.0).astype(BF16)
    high_m = jnp.where(low, 0.0, 1.0).astype(BF16)
    col = lax.broadcasted_iota(jnp.int32, (1, nkeys), 1)
    hrow = lax.broadcasted_iota(jnp.int32, (3 * QB, 1), 0) // QB
    band = band_ref[...]
    head_src = (((0, low_m), (0, high_m), (1, low_m)), ((1, high_m), (2, low_m), (2, high_m)))

    qs, kall, vall, bias, sk = [], [], [], [], []
    for qb in range(TM // QB):
        qrows = slice(qb * QB, (qb + 1) * QB)
        qt = [q_ref[qrows, LANE * t:LANE * (t + 1)] for t in range(AT_DIM // LANE)]
        halves = [(kc_ref, slice(0, QB)), (kc_ref, slice(QB, 2 * QB))]
        if qb == 0:
            blocks = [(kp_ref, slice(0, QB))] + halves
            first_ok, last_ok = i > nct, True
        else:
            blocks = halves + [(kn_ref, slice(0, QB))]
            first_ok, last_ok = True, i < nt - 1
        lo_c = jnp.where(jnp.logical_and(latent, first_ok), 0, jnp.where(latent, QB, nloc))
        hi_c = jnp.where(last_ok, nloc, 2 * QB)
        dead = jnp.logical_and(col < nloc, jnp.logical_or(col < lo_c, col >= hi_c))
        bias_qb = band + jnp.where(dead, MASK_VALUE, 0.0)
        for g in range(2):
            qs.append(jnp.concatenate([qt[t] * hm for t, hm in head_src[g]], axis=0))
            kcols = slice(LANE * g, LANE * (g + 1))
            vcols = slice(2 * LANE + LANE * g, 2 * LANE + LANE * (g + 1))
            kall.append(jnp.concatenate([ref[rs, kcols] for ref, rs in blocks] + [kx_ref[:, kcols]], axis=0))
            vall.append(jnp.concatenate([ref[rs, vcols] for ref, rs in blocks] + [kx_ref[:, vcols]], axis=0))
            bias.append(bias_qb)
            sk.append(LOG2E * jnp.where(hrow == 0, sink_ref[3 * g],
                                        jnp.where(hrow == 1, sink_ref[3 * g + 1], sink_ref[3 * g + 2])))

    n = range(len(qs))
    s = [_mm_nt(qs[c], kall[c]) + bias[c] for c in n]
    mx = [jnp.maximum(jnp.max(s[c], axis=-1, keepdims=True), sk[c]) for c in n]
    p = [jnp.exp2(s[c] - mx[c]) for c in n]
    den = [jnp.sum(p[c], axis=-1, keepdims=True) + jnp.exp2(sk[c] - mx[c]) for c in n]
    o = [_mm(p[c].astype(BF16), vall[c]) / den[c] for c in n]

    blocks = []
    for qb in range(TM // QB):
        o0, o1 = o[2 * qb], o[2 * qb + 1]
        blocks.append(jnp.concatenate([
            jnp.where(low, o0[0:QB], o0[QB:2 * QB]),
            jnp.where(low, o0[2 * QB:3 * QB], o1[0:QB]),
            jnp.where(low, o1[QB:2 * QB], o1[2 * QB:3 * QB]),
        ], axis=1))
    return jnp.concatenate(blocks, axis=0)


def _attn_band(ctx_len):
    t = jnp.arange(3 * QB)[:, None] % QB
    col = jnp.arange(3 * QB + ctx_len)[None, :]
    ok = ((col >= t) & (col <= t + 2 * WINDOW)) | (col >= 3 * QB)
    return jnp.where(ok, 0.0, MASK_VALUE).astype(F32)


def _attn_specs(r, off, ctx_len, band_shape, kvw):
    nct = ctx_len // TM
    per = TM // QB
    nb = r // QB
    clamp = lambda j: jnp.clip(j, nct * per, nb - 1)
    return [
        pl.BlockSpec(memory_space=pltpu.SMEM),
        pl.BlockSpec(band_shape, lambda bi, i: (0, 0)),
        pl.BlockSpec((None, TM, AT_DIM), lambda bi, i: (bi, i + off, 0)),
        pl.BlockSpec((None, QB, kvw), lambda bi, i: (bi, clamp((i + off) * per - 1), 0)),
        pl.BlockSpec((None, TM, kvw), lambda bi, i: (bi, i + off, 0)),
        pl.BlockSpec((None, QB, kvw), lambda bi, i: (bi, clamp((i + off + 1) * per), 0)),
        pl.BlockSpec((None, ctx_len, kvw), lambda bi, i: (bi, 0, 0)),
    ]


def _out_kernel(*refs, nx, tile_off, ctx_len, rows, final):
    x_refs, attn_refs, rest = refs[:nx], refs[nx:nx + 7], refs[nx + 7:]
    fg_ref, o_ref = rest[-2:]
    tile = pl.program_id(1) + tile_off
    x = _stream_tile(x_refs, tile, ctx_len // TM, 1)
    partial, at_gate = _mix_compute(tile * TM, *rest[1:-2], ctx_len, rows)
    y_at = _attn_compute(tile, *attn_refs, ctx_len // TM, rows // TM)
    mod_ref, wout_ref = rest[0], rest[-3]
    acc = partial + _mm(y_at * at_gate, wout_ref[384:768, :])
    xn = x + mod_ref[:, 2 * D_MODEL:3 * D_MODEL] * acc
    if final:
        ms = jnp.mean(xn * xn, axis=-1, keepdims=True)
        xn = xn * lax.rsqrt(ms + NORM_EPS) * fg_ref[...]
    o_ref[...] = xn


def _mix_compute(row0, rw_ref, yf_ref, yb_ref, cv_ref, cvp_ref, cvn_ref,
                 rk_ref, lng_ref, lnb_ref, cvw_ref, wout_ref, ctx_len, rows):
    r = rw_ref[:, 0:384].astype(F32)
    k = rw_ref[:, 384:768].astype(F32)
    v = rw_ref[:, 768:1152].astype(F32)
    g = rw_ref[:, 1152:1536].astype(F32)

    y = yf_ref[...].astype(F32) + yb_ref[...].astype(F32)
    dev = y - _head_sum(y, 1.0 / HEAD_DIM)
    var = _head_sum(dev * dev, 1.0 / HEAD_DIM, passes=1)
    yn = dev * lax.rsqrt(var + LNX_EPS) * lng_ref[...] + lnb_ref[...]
    bonus = _head_sum(r * k * rk_ref[...], passes=1) * v
    y_rw = (yn + bonus) * (g * _sigmoid(g))

    g_at = cv_ref[:, 0:384].astype(F32)
    at_gate = g_at * _sigmoid(g_at)

    bg = cv_ref[:, 384:640].astype(F32)
    u = cv_ref[:, 640:896].astype(F32) * cv_ref[:, 896:1152].astype(F32)
    g_cv = cv_ref[:, 1152:1408].astype(F32)
    prev_ok = jnp.logical_and(row0 != 0, row0 != ctx_len)
    next_ok = jnp.logical_and(row0 + TM != ctx_len, row0 + TM != rows)
    pu = cvp_ref[:, 640:896].astype(F32) * cvp_ref[:, 896:1152].astype(F32)
    nu = cvn_ref[:, 640:896].astype(F32) * cvn_ref[:, 896:1152].astype(F32)
    u_prev = jnp.where(prev_ok, pu[HALO - 1:HALO, :], 0.0)
    u_next = jnp.where(next_ok, nu[0:1, :], 0.0)
    ridx = lax.broadcasted_iota(jnp.int32, (TM, CV_DIM), 0)
    um = jnp.where(ridx == 0, u_prev, pltpu.roll(u, 1, 0))
    up = jnp.where(ridx == TM - 1, u_next, pltpu.roll(u, TM - 1, 0))
    conv = um * cvw_ref[0:1, :] + u * cvw_ref[1:2, :] + up * cvw_ref[2:3, :]
    y_cv = bg * conv * (g_cv * _sigmoid(g_cv))

    return _mm(y_rw, wout_ref[0:384, :]) + _mm(y_cv, wout_ref[768:1024, :]), at_gate


def _out_in_specs(b, r, off, nct, cvw, layer):
    hb = TM // HALO
    row = lambda bi, i: (bi, i + off, 0)
    full = lambda shape: pl.BlockSpec(shape, lambda bi, i: (0,) * len(shape))
    return [
        pl.BlockSpec((None, 1, 3 * D_MODEL), lambda bi, i: (jnp.where(i + off < nct, b, bi), 0, 0)),
        pl.BlockSpec((None, TM, 1536), row),
        pl.BlockSpec((None, TM, RW_DIM), row),
        pl.BlockSpec((None, TM, RW_DIM), lambda bi, i: (bi, _bwd_tile(i + off, nct, r // TM), 0)),
        pl.BlockSpec((None, TM, cvw), row),
        pl.BlockSpec((None, HALO, cvw), lambda bi, i: (bi, jnp.maximum((i + off) * hb - 1, 0), 0)),
        pl.BlockSpec((None, HALO, cvw), lambda bi, i: (bi, jnp.minimum((i + off + 1) * hb, r // HALO - 1), 0)),
        full((1, RW_DIM)), full((1, RW_DIM)), full((1, RW_DIM)),
        full((3, CV_DIM)),
        pl.BlockSpec((None, D_MODEL, D_MODEL), lambda bi, i: (layer, 0, 0)),
    ]


def _out(stream, p_q, p_kv, sink, band, mod, p_rw, y_f, y_b, p_cv, r_k, ln_g, ln_b, cv_w, w_out, layer, final_g,
         nct, ctx_len, final):
    x_ops, split, b, r = _stream_operands(stream, 1)
    nt = r // TM
    off = nct if final else 0
    x_specs = _stream_specs(split, 1, nct, off)
    return pl.pallas_call(
        functools.partial(_out_kernel, nx=len(x_specs), tile_off=off, ctx_len=ctx_len, rows=r, final=final),
        grid=(b, nt - off),
        in_specs=x_specs + _attn_specs(r, off, ctx_len, band.shape, p_kv.shape[-1])
        + _out_in_specs(b, r, off, nct, p_cv.shape[-1], layer)
        + [pl.BlockSpec((1, D_MODEL), lambda bi, i: (0, 0))],
        out_specs=pl.BlockSpec((None, TM, D_MODEL), lambda bi, i: (bi, i, 0)),
        out_shape=jax.ShapeDtypeStruct((b, r - off * TM, D_MODEL), F32),
        compiler_params=pltpu.CompilerParams(
            dimension_semantics=("parallel", "parallel"), vmem_limit_bytes=VMEM_LIMIT),
        name="out",
    )(*x_ops, sink, band, p_q, p_kv, p_kv, p_kv, p_kv,
      mod, p_rw, y_f, y_b, p_cv, p_cv, p_cv, r_k, ln_g, ln_b, cv_w, w_out, final_g)


def _rope_tables(seq, ctx_len):
    t = jnp.arange(seq)
    nfreq = HEAD_DIM // 4
    inv = ROPE_THETA ** (-jnp.arange(nfreq, dtype=F32) / nfreq)
    ang = jnp.concatenate([(t // GRID_W).astype(F32)[:, None] * inv,
                           (t % GRID_W).astype(F32)[:, None] * inv], axis=-1)
    cos, sin = jnp.cos(ang), jnp.sin(ang)
    reps = LANE // HEAD_DIM
    cos_t = jnp.tile(jnp.concatenate([cos, cos], axis=-1), (1, reps))
    sin_t = jnp.tile(jnp.concatenate([-sin, sin], axis=-1), (1, reps))
    cos_t = jnp.concatenate([jnp.ones((ctx_len, LANE), F32), cos_t], axis=0)
    sin_t = jnp.concatenate([jnp.zeros((ctx_len, LANE), F32), sin_t], axis=0)
    return cos_t, sin_t


def kernel(x, c, ctx, c_ctx, w_ada, b_ada, norm_g, w_in, w_out, rw_w0, rw_w_up, rw_a0, rw_a_up,
           rw_k_k, rw_k_a, rw_r_k, rw_ln_g, rw_ln_b, at_sink, cv_w, final_g):
    b, seq, _ = x.shape
    ctx_len = ctx.shape[1]
    depth = w_in.shape[0]
    assert ctx_len % TM == 0 and seq % TM == 0 and b + 1 <= 8
    nct = ctx_len // TM

    xa = (x, ctx)
    cvec = jnp.zeros((8, D_MODEL), F32).at[0:b].set(c).at[b].set(c_ctx)
    mod = _ada(cvec, w_ada, b_ada)
    cos_t, sin_t = _rope_tables(seq, ctx_len)
    band = _attn_band(ctx_len)

    for l in range(depth):
        mod_l = mod[l].reshape(8, 1, 3 * D_MODEL)
        p_rw, p_q, p_kv, p_cv = _proj(xa, mod_l, norm_g[l].reshape(1, -1), w_in, l, cos_t, sin_t, ctx_len)
        y_f, y_b = _rwkv(p_rw, rw_w0[l], rw_w_up[l].reshape(2 * HEAD_DIM, RW_DIM), rw_a0[l],
                         rw_a_up[l].reshape(2 * HEAD_DIM, RW_DIM), rw_k_k[l].reshape(1, -1),
                         rw_k_a[l].reshape(1, -1), nct)
        xa = _out(xa, p_q, p_kv, at_sink[l], band, mod_l, p_rw, y_f, y_b, p_cv, rw_r_k[l].reshape(1, -1),
                  rw_ln_g[l].reshape(1, -1), rw_ln_b[l].reshape(1, -1), cv_w[l], w_out, l,
                  final_g.reshape(1, -1), nct, ctx_len, final=(l == depth - 1))
    return xa
```

```python
import functools

import jax
import jax.numpy as jnp
from jax import lax
from jax.experimental import pallas as pl
from jax.experimental.pallas import tpu as pltpu

F32 = jnp.float32
BF16 = jnp.bfloat16

D_MODEL = 1024
HEAD_DIM = 64
GRID_W = 64
ROPE_THETA = 10000.0
RW_DIM = 384
AT_DIM = 384
CV_DIM = 256
WINDOW = 128
NORM_EPS = 1e-6
LNX_EPS = 64e-5
MASK_VALUE = -1e30
DECAY_SCALE = 0.6065306597126334
LOG2E = 1.4426950408889634
Q_SCALE = HEAD_DIM ** -0.5 * LOG2E

COL_RW = (0, 1792)
COL_Q = (1792, 2176)
COL_KV = (2176, 2432)
COL_CV = (2432, 3840)
D_IN = 3840

TM = 256
PROJ_TILES = (768, 512, 256)
CHUNK = 64
RW_TILES = 3
QB = 128
LANE = 128
HALO = 16
GROUP_W = 256
VMEM_LIMIT = 56 * 1024 * 1024


def _mm(a, b):
    return jnp.dot(a, b, preferred_element_type=F32)


def _mm_nt(a, b):
    return lax.dot_general(a, b, (((1,), (1,)), ((), ())), preferred_element_type=F32)


def _dot_split(a, b, passes=2, left_exact=False):
    acc = None
    rem = a if not left_exact else b
    for p in range(passes):
        hi = rem.astype(BF16)
        t = jnp.dot(a, hi, preferred_element_type=F32) if left_exact else jnp.dot(hi, b, preferred_element_type=F32)
        acc = t if acc is None else acc + t
        if p + 1 < passes:
            rem = rem - hi.astype(F32)
    return acc


def _sigmoid(x):
    return 1.0 / (1.0 + jnp.exp(-x))


def _head_ones(n, scale=1.0):
    r = lax.broadcasted_iota(jnp.int32, (n, n), 0) // HEAD_DIM
    c = lax.broadcasted_iota(jnp.int32, (n, n), 1) // HEAD_DIM
    return jnp.where(r == c, scale, 0.0).astype(BF16)


def _head_sum(x, scale=1.0, passes=2):
    parts = []
    for lo in range(0, x.shape[1], GROUP_W):
        hi = min(lo + GROUP_W, x.shape[1])
        parts.append(_dot_split(x[:, lo:hi], _head_ones(hi - lo, scale), passes))
    return jnp.concatenate(parts, axis=1)


def _ada_kernel(c_ref, w_ref, b_ref, o_ref):
    c = c_ref[...]
    a = c * _sigmoid(c)
    w = w_ref[...]
    a_hi = a.astype(BF16)
    w_hi = w.astype(BF16)
    a_lo = (a - a_hi.astype(F32)).astype(BF16)
    w_lo = (w - w_hi.astype(F32)).astype(BF16)
    o_ref[...] = _mm(a_hi, w_hi) + _mm(a_lo, w_hi) + _mm(a_hi, w_lo) + b_ref[...]


def _ada(cvec, w_ada, b_ada):
    depth = w_ada.shape[0]
    tn = 1024
    return pl.pallas_call(
        _ada_kernel,
        grid=(depth, 3 * D_MODEL // tn),
        in_specs=[
            pl.BlockSpec((8, D_MODEL), lambda l, n: (0, 0)),
            pl.BlockSpec((None, D_MODEL, tn), lambda l, n: (l, 0, n)),
            pl.BlockSpec((None, 1, tn), lambda l, n: (l, 0, n)),
        ],
        out_specs=pl.BlockSpec((None, 8, tn), lambda l, n: (l, 0, n)),
        out_shape=jax.ShapeDtypeStruct((depth, 8, 3 * D_MODEL), F32),
        compiler_params=pltpu.CompilerParams(vmem_limit_bytes=VMEM_LIMIT),
        name="ada",
    )(cvec, w_ada, b_ada.reshape(depth, 1, 3 * D_MODEL))


def _stream_tile(x_refs, tile0, nct, nsub):
    if len(x_refs) == 1:
        return x_refs[0][...]
    blocks = []
    for k in range(nsub):
        blk = x_refs[k][...]
        if nsub + k < len(x_refs):
            blk = jnp.where(tile0 + k < nct, x_refs[nsub + k][...], blk)
        blocks.append(blk)
    return blocks[0] if nsub == 1 else jnp.concatenate(blocks, axis=0)


def _stream_specs(split, nsub, nct, off=0):
    if not split:
        return [pl.BlockSpec((None, nsub * TM, D_MODEL), lambda bi, i: (bi, i + off, 0))]
    lat = [pl.BlockSpec((None, TM, D_MODEL),
                        lambda bi, i, k=k: (bi, jnp.maximum((i + off) * nsub + k - nct, 0), 0)) for k in range(nsub)]
    ctx = [pl.BlockSpec((None, TM, D_MODEL),
                        lambda bi, i, k=k: (bi, jnp.minimum((i + off) * nsub + k, nct - 1), 0))
           for k in range(min(nsub, nct))]
    return lat + ctx


def _proj_kernel(*refs, ctx_len, nx):
    x_refs, (modb_ref, modc_ref, g_ref, w_ref, cos_ref, sin_ref, rw_ref, q_ref, kv_ref, cv_ref) = refs[:nx], refs[nx:]
    tm = rw_ref.shape[0]
    i = pl.program_id(1)
    x = _stream_tile(x_refs, i * (tm // TM), ctx_len // TM, tm // TM)
    _proj_compute(x, i * tm, modb_ref, modc_ref, g_ref, w_ref, cos_ref, sin_ref,
                  rw_ref, q_ref, kv_ref, cv_ref, ctx_len)


def _proj_compute(x, row0, modb_ref, modc_ref, g_ref, w_ref, cos_ref, sin_ref, rw_ref, q_ref, kv_ref, cv_ref,
                  ctx_len):
    tm = x.shape[0]
    ms = jnp.mean(x * x, axis=-1, keepdims=True)
    y = x * lax.rsqrt(ms + NORM_EPS) * g_ref[...]
    mb = modb_ref[...]
    mc = modc_ref[...]
    blocks = []
    for k in range(tm // TM):
        is_ctx = row0 + k * TM < ctx_len
        shift = jnp.where(is_ctx, mc[:, 0:D_MODEL], mb[:, 0:D_MODEL])
        gain = 1.0 + jnp.where(is_ctx, mc[:, D_MODEL:2 * D_MODEL], mb[:, D_MODEL:2 * D_MODEL])
        blocks.append(y[k * TM:(k + 1) * TM] * gain + shift)
    h = blocks[0] if len(blocks) == 1 else jnp.concatenate(blocks, axis=0)

    def seg(cols):
        return jnp.dot(h, w_ref[:, cols[0]:cols[1]], preferred_element_type=F32)

    cos = cos_ref[...]
    sin = sin_ref[...]
    lane = lax.broadcasted_iota(jnp.int32, (tm, LANE), 1)
    first_half = (lane % HEAD_DIM) < (HEAD_DIM // 2)
    low_head = lane < HEAD_DIM

    def rope(t):
        rot = jnp.where(first_half, pltpu.roll(t, LANE - 32, 1), pltpu.roll(t, 32, 1))
        return t * cos + rot * sin

    def dup(t):
        sw = pltpu.roll(t, HEAD_DIM, 1)
        return jnp.where(low_head, t, sw), jnp.where(low_head, sw, t)

    q = seg(COL_Q)
    q_ref[...] = jnp.concatenate(
        [rope(q[:, LANE * j:LANE * (j + 1)]) * Q_SCALE for j in range(AT_DIM // LANE)],
        axis=1).astype(BF16)
    kv = seg(COL_KV)
    k0, k1 = dup(rope(kv[:, 0:LANE]))
    v0, v1 = dup(kv[:, LANE:2 * LANE])
    kv_ref[...] = jnp.concatenate([k0, k1, v0, v1], axis=1).astype(BF16)
    rw_ref[...] = seg(COL_RW).astype(BF16)
    cv_ref[...] = seg(COL_CV).astype(BF16)


def _stream_operands(stream, nsub):
    if isinstance(stream, tuple):
        x, ctx = stream
        return [x] * nsub + [ctx] * min(nsub, ctx.shape[1] // TM), True, x.shape[0], x.shape[1] + ctx.shape[1]
    return [stream], False, stream.shape[0], stream.shape[1]


def _proj(stream, mod, norm_g, w_in, layer, cos_t, sin_t, ctx_len):
    _, split, b, r = _stream_operands(stream, 1)
    tm = next(t for t in PROJ_TILES if r % t == 0)
    nsub = tm // TM
    x_ops = _stream_operands(stream, nsub)[0]
    x_specs = _stream_specs(split, nsub, ctx_len // TM)
    row = lambda bi, i: (bi, i, 0)
    widths = (COL_RW[1] - COL_RW[0], AT_DIM, 4 * LANE, COL_CV[1] - COL_CV[0])
    return pl.pallas_call(
        functools.partial(_proj_kernel, ctx_len=ctx_len, nx=len(x_specs)),
        grid=(b, r // tm),
        in_specs=x_specs + [
            pl.BlockSpec((None, 1, 3 * D_MODEL), lambda bi, i: (bi, 0, 0)),
            pl.BlockSpec((None, 1, 3 * D_MODEL), lambda bi, i: (b, 0, 0)),
            pl.BlockSpec((1, D_MODEL), lambda bi, i: (0, 0)),
            pl.BlockSpec((None, D_MODEL, D_IN), lambda bi, i: (layer, 0, 0), pipeline_mode=pl.Buffered(1)),
            pl.BlockSpec((tm, LANE), lambda bi, i: (i, 0)),
            pl.BlockSpec((tm, LANE), lambda bi, i: (i, 0)),
        ],
        out_specs=[pl.BlockSpec((None, tm, w), row) for w in widths],
        out_shape=[jax.ShapeDtypeStruct((b, r, w), BF16) for w in widths],
        compiler_params=pltpu.CompilerParams(
            dimension_semantics=("parallel", "parallel"), vmem_limit_bytes=VMEM_LIMIT),
        name="proj",
    )(*x_ops, mod, mod, norm_g, w_in, cos_t, sin_t)


_RW_FIELDS = ("at", "rt", "kt", "bt", "kh", "bh", "v")


def _group_masks(rev_lo, rev_hi):
    w = GROUP_W
    row = lax.broadcasted_iota(jnp.int32, (CHUNK, w), 0)
    lane = lax.broadcasted_iota(jnp.int32, (CHUNK, w), 1)
    s = lane % CHUNK
    x = row ^ s
    if rev_lo == rev_hi:
        before = (row - s) if rev_lo else (s - row)
    else:
        sign = jnp.where(lane >= LANE, -1 if rev_hi else 1, -1 if rev_lo else 1)
        before = (s - row) * sign
    m = {
        "strict": before < 0,
        "incl": before <= 0,
        "eye": s == row,
        "lvl1": (x >> 1) == 0,
        "heads": [(lane // HEAD_DIM) == h for h in range(w // HEAD_DIM)],
    }
    for lg in range(1, 6):
        m["lvl%d" % (1 << lg)] = (x >> lg) == 1
    return m


def _bd(x, m):
    return jnp.concatenate([jnp.where(hm, x, 0.0) for hm in m["heads"]], axis=0)


def _head_transpose(x):
    xt = x.T
    return jnp.concatenate([xt[h * HEAD_DIM:(h + 1) * HEAD_DIM] for h in range(x.shape[1] // HEAD_DIM)], axis=1)


def _chunk_pieces(ch, side=()):
    def advance():
        for gen in side:
            next(gen, None)

    n = range(len(ch))
    ms = [c["m"] for c in ch]
    advance()
    ar = [jnp.concatenate([c["at"], c["rt"]], axis=0) for c in ch]
    gb = [_mm(ar[i], _bd(_head_transpose(ch[i]["bt"]), ms[i])) for i in n]
    gk = [_mm(ar[i], _bd(_head_transpose(ch[i]["kt"]), ms[i])) for i in n]
    lo = [jnp.where(ms[i]["strict"], gb[i][:CHUNK], 0.0) for i in n]
    aak = [jnp.where(ms[i]["strict"], gk[i][:CHUNK], 0.0) for i in n]
    arb = [jnp.where(ms[i]["incl"], gb[i][CHUNK:], 0.0) for i in n]
    ark = [jnp.where(ms[i]["incl"], gk[i][CHUNK:], 0.0) for i in n]

    t = [jnp.where(ms[i]["eye"], 1.0, 0.0) - jnp.where(ms[i]["lvl1"], lo[i], 0.0) for i in n]
    for lvl in (2, 4, 8, 16, 32):
        x = [_mm(t[i], _bd(jnp.where(ms[i]["lvl%d" % lvl], lo[i], 0.0), ms[i])) for i in n]
        t = [t[i] - _mm(x[i], _bd(t[i], ms[i])) for i in n]
        advance()

    c2 = 2 * CHUNK
    bht = [_head_transpose(c["bh"]) for c in ch]
    kht = [_head_transpose(c["kh"]) for c in ch]
    av = [_mm(jnp.concatenate([aak[i], ark[i], kht[i]], axis=0), _bd(ch[i]["v"], ms[i])) for i in n]
    z = [_mm(jnp.concatenate([arb[i], bht[i]], axis=0), _bd(t[i], ms[i])) for i in n]
    pw = [_mm(z[i], _bd(ch[i]["at"], ms[i])) for i in n]
    pu = [_mm(z[i], _bd(av[i][:CHUNK], ms[i])) for i in n]
    qh = [ch[i]["rt"] - pw[i][:CHUNK] for i in n]
    yi = [av[i][CHUNK:c2] - pu[i][:CHUNK] for i in n]
    mm = [jnp.where(ms[i]["eye"], ch[i]["pe"], 0.0) - pw[i][CHUNK:] for i in n]
    nn = [av[i][c2:] - pu[i][CHUNK:] for i in n]
    return qh, yi, mm, nn


_RW_SLOT_FIELDS = _RW_FIELDS + ("pe",)
_RW_LAYOUT = (((0, 0, 256),), ((0, 256, 384), (1, 0, 128)), ((1, 128, 384),))


def _rwkv_prep(x_ref, d, w0_ref, wup_ref, a0_ref, aup_ref, kk_ref, ka_ref, dst):
    nchunk = TM // CHUNK
    rev = d == 1
    rt_i = lax.broadcasted_iota(jnp.int32, (TM, TM), 0)
    ct_i = lax.broadcasted_iota(jnp.int32, (TM, TM), 1)
    same_chunk = (rt_i // CHUNK) == (ct_i // CHUNK)
    lane = lax.broadcasted_iota(jnp.int32, (TM, LANE), 1)
    r = x_ref[:, 0:384].astype(F32)
    k = x_ref[:, 384:768].astype(F32)
    dmask = (lane // HEAD_DIM) == d
    tw = jnp.where(dmask, jnp.tanh(x_ref[:, 1536:1664].astype(F32)), 0.0)
    ta = jnp.where(dmask, x_ref[:, 1664:1792].astype(F32), 0.0)
    lw = w0_ref[d:d + 1, :] + _mm(tw, wup_ref[...])
    la = a0_ref[d:d + 1, :] + _mm(ta, aup_ref[...])
    kk = k * kk_ref[...]
    ss = _head_sum(kk * kk, passes=1)
    yield
    logw = -(DECAY_SCALE * LOG2E) * _sigmoid(lw)
    tri = jnp.where(same_chunk & ((ct_i >= rt_i) if rev else (ct_i <= rt_i)), 1.0, 0.0).astype(BF16)
    cl = _dot_split(tri, logw, left_exact=True)
    yield
    a = _sigmoid(la)
    kk = kk * lax.rsqrt(jnp.maximum(ss, 1e-12))
    kd = k * (1.0 + (a - 1.0) * ka_ref[...])
    beta = kk * a
    ends = [(c * CHUNK) if rev else (c * CHUNK + CHUNK - 1) for c in range(nchunk)]
    pe = jnp.concatenate([jnp.broadcast_to(jnp.exp2(cl[e:e + 1, :]), (CHUNK, RW_DIM)) for e in ends], axis=0)
    inv = jnp.exp2(-cl)
    kt = kd * inv
    bt = beta * inv
    dst["at"][...] = kk * jnp.exp2(cl - logw)
    dst["rt"][...] = r * jnp.exp2(cl)
    dst["kt"][...] = kt
    dst["bt"][...] = bt
    dst["kh"][...] = kt * pe
    dst["bh"][...] = bt * pe
    dst["v"][...] = x_ref[:, 768:1152].astype(F32)
    dst["pe"][...] = pe


def _rwkv_chains(src, masks, groups, row_off):
    nchunk = TM // CHUNK
    chains = []
    for step in range(nchunk):
        rows = (slice(step * CHUNK, (step + 1) * CHUNK),
                slice((nchunk - 1 - step) * CHUNK, (nchunk - step) * CHUNK))
        for g in groups:
            parts = _RW_LAYOUT[g]

            def gather(name):
                cols = [src[d][name][rows[d], lo:hi] for d, lo, hi in parts]
                return cols[0] if len(cols) == 1 else jnp.concatenate(cols, axis=1)
            ch = {name: gather(name) for name in _RW_FIELDS}
            ch.update(pe=gather("pe")[0:1, :], m=masks[g], g=g, step=step,
                      dst=[(d, slice(rows[d].start + row_off, rows[d].stop + row_off), lo, hi)
                           for d, lo, hi in parts])
            chains.append(ch)
    return chains


def _rwkv_state_pass(chains, pieces, yf_ref, yb_ref, h_refs):
    qh, yi, mm, nn = pieces
    y_refs = (yf_ref, yb_ref)
    h = [h_ref[...] for h_ref in h_refs]
    for step in range(TM // CHUNK):
        idx = [i for i, c in enumerate(chains) if c["step"] == step]
        res = {i: _mm(jnp.concatenate([qh[i], mm[i]], axis=0), _bd(h[chains[i]["g"]], chains[i]["m"]))
               for i in idx}
        for i in idx:
            ys = res[i][:CHUNK] + yi[i]
            off = 0
            for d, rws, lo, hi in chains[i]["dst"]:
                y_refs[d][rws, lo:hi] = ys[:, off:off + hi - lo].astype(yf_ref.dtype)
                off += hi - lo
            h[chains[i]["g"]] = res[i][CHUNK:] + nn[i]
        yield
    for h_ref, hv in zip(h_refs, h):
        h_ref[...] = hv


def _drain(*gens):
    for gen in gens:
        for _ in gen:
            pass


def _rwkv_kernel(*refs):
    x_refs, refs = refs[:2 * RW_TILES], refs[2 * RW_TILES:]
    params, (yf_ref, yb_ref), scratch = refs[:6], refs[6:8], refs[8:]
    ngroup = len(_RW_LAYOUT)
    h_refs = scratch[:ngroup]
    nf = len(_RW_SLOT_FIELDS)
    prepared = [[dict(zip(_RW_SLOT_FIELDS, scratch[ngroup + (t * 2 + d) * nf:ngroup + (t * 2 + d + 1) * nf]))
                 for d in range(2)] for t in range(RW_TILES)]

    @pl.when(pl.program_id(1) == 0)
    def _():
        for ref in h_refs:
            ref[...] = jnp.zeros_like(ref)

    masks = [_group_masks(False, False), _group_masks(False, True), _group_masks(True, True)]

    def prep(t, d):
        return _rwkv_prep(x_refs[d * RW_TILES + t], d, *params, prepared[t][d])

    _drain(prep(0, 0))
    pending = ()
    for t in range(RW_TILES):
        side = (prep(t, 1),) + pending
        chains_a = _rwkv_chains(prepared[t], masks, (0,), t * TM)
        pieces_a = _chunk_pieces(chains_a, side)
        _drain(*side)
        side = (prep(t + 1, 0),) if t + 1 < RW_TILES else ()
        chains_b = _rwkv_chains(prepared[t], masks, (1, 2), t * TM)
        pieces_b = _chunk_pieces(chains_b, side)
        _drain(*side)
        pieces = [a + b for a, b in zip(pieces_a, pieces_b)]
        pending = (_rwkv_state_pass(chains_a + chains_b, pieces, yf_ref, yb_ref, h_refs),)
    _drain(*pending)


def _bwd_tile(i, nct, nt):
    return jnp.where(i < nct, nct - 1 - i, nt - 1 - (i - nct))


def _rwkv(p_rw, w0, wup, a0, aup, k_k, k_a, nct):
    b, r, wcols = p_rw.shape
    nt = r // TM
    steps = -(-nt // RW_TILES)
    src = []
    for d in range(2):
        for t in range(RW_TILES):
            def index(bi, j, t=t, d=d):
                tile = jnp.minimum(j * RW_TILES + t, nt - 1)
                return (bi, _bwd_tile(tile, nct, nt) if d else tile, 0)
            src.append(pl.BlockSpec((None, TM, wcols), index))
    full = lambda shape: pl.BlockSpec(shape, lambda bi, j: (0,) * len(shape))
    dst = pl.BlockSpec((None, RW_TILES * TM, RW_DIM), lambda bi, j: (bi, j, 0))
    scratch = [pltpu.VMEM((HEAD_DIM, GROUP_W), F32) for _ in _RW_LAYOUT]
    scratch += [pltpu.VMEM((TM, RW_DIM), F32) for _ in range(RW_TILES * 2 * len(_RW_SLOT_FIELDS))]
    return pl.pallas_call(
        _rwkv_kernel,
        grid=(b, steps),
        in_specs=src + [
            full((2, RW_DIM)), full((2 * HEAD_DIM, RW_DIM)),
            full((2, RW_DIM)), full((2 * HEAD_DIM, RW_DIM)),
            full((1, RW_DIM)), full((1, RW_DIM)),
        ],
        out_specs=[dst, dst],
        out_shape=[jax.ShapeDtypeStruct((b, steps * RW_TILES * TM, RW_DIM), BF16)] * 2,
        scratch_shapes=scratch,
        compiler_params=pltpu.CompilerParams(
            dimension_semantics=("parallel", "arbitrary"), vmem_limit_bytes=VMEM_LIMIT),
        name="rwkv",
    )(*([p_rw] * (2 * RW_TILES)), w0, wup, a0, aup, k_k, k_a)


def _attn_compute(i, sink_ref, band_ref, q_ref, kp_ref, kc_ref, kn_ref, kx_ref, nct, nt):
    latent = i >= nct
    nloc = 3 * QB
    nkeys = nloc + kx_ref.shape[0]

    lane = lax.broadcasted_iota(jnp.int32, (QB, LANE), 1)
    low = lane < HEAD_DIM
    low_m = jnp.where(low, 1.0, 0.0).astype(BF16)
    high_m = jnp.where(low, 0.0, 1.0).astype(BF16)
    key = lax.broadcasted_iota(jnp.int32, (nkeys, 1), 0)
    hrow = lax.broadcasted_iota(jnp.int32, (3 * QB, 1), 0) // QB
    band = band_ref[...]
    t_row = lax.broadcasted_iota(jnp.int32, (3 * QB, LANE), 0) % QB
    t_hot = jnp.where(t_row == lax.broadcasted_iota(jnp.int32, (3 * QB, LANE), 1), 1.0, 0.0).astype(BF16)
    head_src = (((0, low_m), (0, high_m), (1, low_m)), ((1, high_m), (2, low_m), (2, high_m)))

    qs, kall, vall, sk = [], [], [], []
    for qb in range(TM // QB):
        qrows = slice(qb * QB, (qb + 1) * QB)
        qt = [q_ref[qrows, LANE * t:LANE * (t + 1)] for t in range(AT_DIM // LANE)]
        halves = [(kc_ref, slice(0, QB)), (kc_ref, slice(QB, 2 * QB))]
        if qb == 0:
            blocks = [(kp_ref, slice(0, QB))] + halves
            first_ok, last_ok = i > nct, True
        else:
            blocks = halves + [(kn_ref, slice(0, QB))]
            first_ok, last_ok = True, i < nt - 1
        lo_c = jnp.where(jnp.logical_and(latent, first_ok), 0, jnp.where(latent, QB, nloc))
        hi_c = jnp.where(last_ok, nloc, 2 * QB)
        dead = jnp.logical_and(key < nloc, jnp.logical_or(key < lo_c, key >= hi_c))
        bias_qb = (band + jnp.where(dead, MASK_VALUE, 0.0)).astype(BF16)
        for g in range(2):
            qs.append(jnp.concatenate(
                [jnp.concatenate([qt[t] * hm for t, hm in head_src[g]], axis=0), t_hot], axis=1))
            kcols = slice(LANE * g, LANE * (g + 1))
            vcols = slice(2 * LANE + LANE * g, 2 * LANE + LANE * (g + 1))
            kall.append(jnp.concatenate(
                [jnp.concatenate([ref[rs, kcols] for ref, rs in blocks] + [kx_ref[:, kcols]], axis=0), bias_qb],
                axis=1))
            vall.append(jnp.concatenate([ref[rs, vcols] for ref, rs in blocks] + [kx_ref[:, vcols]], axis=0))
            sk.append(LOG2E * jnp.where(hrow == 0, sink_ref[3 * g],
                                        jnp.where(hrow == 1, sink_ref[3 * g + 1], sink_ref[3 * g + 2])))

    n = range(len(qs))
    s = [_mm_nt(qs[c], kall[c]) for c in n]
    mx = [jnp.maximum(jnp.max(s[c], axis=-1, keepdims=True), sk[c]) for c in n]
    p = [jnp.exp2(s[c] - mx[c]) for c in n]
    den = [jnp.sum(p[c], axis=-1, keepdims=True) + jnp.exp2(sk[c] - mx[c]) for c in n]
    o = [_mm(p[c].astype(BF16), vall[c]) / den[c] for c in n]

    blocks = []
    for qb in range(TM // QB):
        o0, o1 = o[2 * qb], o[2 * qb + 1]
        blocks.append(jnp.concatenate([
            jnp.where(low, o0[0:QB], o0[QB:2 * QB]),
            jnp.where(low, o0[2 * QB:3 * QB], o1[0:QB]),
            jnp.where(low, o1[QB:2 * QB], o1[2 * QB:3 * QB]),
        ], axis=1))
    return jnp.concatenate(blocks, axis=0)


def _attn_band(ctx_len):
    key = jnp.arange(3 * QB + ctx_len)[:, None]
    t = jnp.arange(QB)[None, :]
    ok = ((key >= t) & (key <= t + 2 * WINDOW)) | (key >= 3 * QB)
    return jnp.where(ok, 0.0, MASK_VALUE).astype(F32)


def _attn_specs(r, off, ctx_len, band_shape, kvw):
    nct = ctx_len // TM
    per = TM // QB
    nb = r // QB
    clamp = lambda j: jnp.clip(j, nct * per, nb - 1)
    return [
        pl.BlockSpec(memory_space=pltpu.SMEM),
        pl.BlockSpec(band_shape, lambda bi, i: (0, 0)),
        pl.BlockSpec((None, TM, AT_DIM), lambda bi, i: (bi, i + off, 0)),
        pl.BlockSpec((None, QB, kvw), lambda bi, i: (bi, clamp((i + off) * per - 1), 0)),
        pl.BlockSpec((None, TM, kvw), lambda bi, i: (bi, i + off, 0)),
        pl.BlockSpec((None, QB, kvw), lambda bi, i: (bi, clamp((i + off + 1) * per), 0)),
        pl.BlockSpec((None, ctx_len, kvw), lambda bi, i: (bi, 0, 0)),
    ]


def _out_kernel(*refs, nx, tile_off, ctx_len, rows, final):
    x_refs, attn_refs, rest = refs[:nx], refs[nx:nx + 7], refs[nx + 7:]
    fg_ref, o_ref = rest[-2:]
    tile = pl.program_id(1) + tile_off
    x = _stream_tile(x_refs, tile, ctx_len // TM, 1)
    partial, at_gate = _mix_compute(tile * TM, *rest[1:-2], ctx_len, rows)
    y_at = _attn_compute(tile, *attn_refs, ctx_len // TM, rows // TM)
    mod_ref, wout_ref = rest[0], rest[-3]
    acc = partial + _mm(y_at * at_gate, wout_ref[384:768, :])
    xn = x + mod_ref[:, 2 * D_MODEL:3 * D_MODEL] * acc
    if final:
        ms = jnp.mean(xn * xn, axis=-1, keepdims=True)
        xn = xn * lax.rsqrt(ms + NORM_EPS) * fg_ref[...]
    o_ref[...] = xn


def _mix_compute(row0, rw_ref, yf_ref, yb_ref, cv_ref, cvp_ref, cvn_ref,
                 rk_ref, lng_ref, lnb_ref, cvw_ref, wout_ref, ctx_len, rows):
    r = rw_ref[:, 0:384].astype(F32)
    k = rw_ref[:, 384:768].astype(F32)
    v = rw_ref[:, 768:1152].astype(F32)
    g = rw_ref[:, 1152:1536].astype(F32)

    y = yf_ref[...].astype(F32) + yb_ref[...].astype(F32)
    dev = y - _head_sum(y, 1.0 / HEAD_DIM)
    var = _head_sum(dev * dev, 1.0 / HEAD_DIM, passes=1)
    yn = dev * lax.rsqrt(var + LNX_EPS) * lng_ref[...] + lnb_ref[...]
    bonus = _head_sum(r * k * rk_ref[...], passes=1) * v
    y_rw = (yn + bonus) * (g * _sigmoid(g))

    g_at = cv_ref[:, 0:384].astype(F32)
    at_gate = g_at * _sigmoid(g_at)

    bg = cv_ref[:, 384:640].astype(F32)
    u = cv_ref[:, 640:896].astype(F32) * cv_ref[:, 896:1152].astype(F32)
    g_cv = cv_ref[:, 1152:1408].astype(F32)
    prev_ok = jnp.logical_and(row0 != 0, row0 != ctx_len)
    next_ok = jnp.logical_and(row0 + TM != ctx_len, row0 + TM != rows)
    pu = cvp_ref[:, 640:896].astype(F32) * cvp_ref[:, 896:1152].astype(F32)
    nu = cvn_ref[:, 640:896].astype(F32) * cvn_ref[:, 896:1152].astype(F32)
    u_prev = jnp.where(prev_ok, pu[HALO - 1:HALO, :], 0.0)
    u_next = jnp.where(next_ok, nu[0:1, :], 0.0)
    ridx = lax.broadcasted_iota(jnp.int32, (TM, CV_DIM), 0)
    um = jnp.where(ridx == 0, u_prev, pltpu.roll(u, 1, 0))
    up = jnp.where(ridx == TM - 1, u_next, pltpu.roll(u, TM - 1, 0))
    conv = um * cvw_ref[0:1, :] + u * cvw_ref[1:2, :] + up * cvw_ref[2:3, :]
    y_cv = bg * conv * (g_cv * _sigmoid(g_cv))

    return _mm(y_rw, wout_ref[0:384, :]) + _mm(y_cv, wout_ref[768:1024, :]), at_gate


def _out_in_specs(b, r, off, nct, cvw, layer):
    hb = TM // HALO
    row = lambda bi, i: (bi, i + off, 0)
    full = lambda shape: pl.BlockSpec(shape, lambda bi, i: (0,) * len(shape))
    return [
        pl.BlockSpec((None, 1, 3 * D_MODEL), lambda bi, i: (jnp.where(i + off < nct, b, bi), 0, 0)),
        pl.BlockSpec((None, TM, 1536), row),
        pl.BlockSpec((None, TM, RW_DIM), row),
        pl.BlockSpec((None, TM, RW_DIM), lambda bi, i: (bi, _bwd_tile(i + off, nct, r // TM), 0)),
        pl.BlockSpec((None, TM, cvw), row),
        pl.BlockSpec((None, HALO, cvw), lambda bi, i: (bi, jnp.maximum((i + off) * hb - 1, 0), 0)),
        pl.BlockSpec((None, HALO, cvw), lambda bi, i: (bi, jnp.minimum((i + off + 1) * hb, r // HALO - 1), 0)),
        full((1, RW_DIM)), full((1, RW_DIM)), full((1, RW_DIM)),
        full((3, CV_DIM)),
        pl.BlockSpec((None, D_MODEL, D_MODEL), lambda bi, i: (layer, 0, 0)),
    ]


def _out(stream, p_q, p_kv, sink, band, mod, p_rw, y_f, y_b, p_cv, r_k, ln_g, ln_b, cv_w, w_out, layer, final_g,
         nct, ctx_len, final):
    x_ops, split, b, r = _stream_operands(stream, 1)
    nt = r // TM
    off = nct if final else 0
    x_specs = _stream_specs(split, 1, nct, off)
    return pl.pallas_call(
        functools.partial(_out_kernel, nx=len(x_specs), tile_off=off, ctx_len=ctx_len, rows=r, final=final),
        grid=(b, nt - off),
        in_specs=x_specs + _attn_specs(r, off, ctx_len, band.shape, p_kv.shape[-1])
        + _out_in_specs(b, r, off, nct, p_cv.shape[-1], layer)
        + [pl.BlockSpec((1, D_MODEL), lambda bi, i: (0, 0))],
        out_specs=pl.BlockSpec((None, TM, D_MODEL), lambda bi, i: (bi, i, 0)),
        out_shape=jax.ShapeDtypeStruct((b, r - off * TM, D_MODEL), F32),
        compiler_params=pltpu.CompilerParams(
            dimension_semantics=("parallel", "parallel"), vmem_limit_bytes=VMEM_LIMIT),
        name="out",
    )(*x_ops, sink, band, p_q, p_kv, p_kv, p_kv, p_kv,
      mod, p_rw, y_f, y_b, p_cv, p_cv, p_cv, r_k, ln_g, ln_b, cv_w, w_out, final_g)


def _rope_tables(seq, ctx_len):
    t = jnp.arange(seq)
    nfreq = HEAD_DIM // 4
    inv = ROPE_THETA ** (-jnp.arange(nfreq, dtype=F32) / nfreq)
    ang = jnp.concatenate([(t // GRID_W).astype(F32)[:, None] * inv,
                           (t % GRID_W).astype(F32)[:, None] * inv], axis=-1)
    cos, sin = jnp.cos(ang), jnp.sin(ang)
    reps = LANE // HEAD_DIM
    cos_t = jnp.tile(jnp.concatenate([cos, cos], axis=-1), (1, reps))
    sin_t = jnp.tile(jnp.concatenate([-sin, sin], axis=-1), (1, reps))
    cos_t = jnp.concatenate([jnp.ones((ctx_len, LANE), F32), cos_t], axis=0)
    sin_t = jnp.concatenate([jnp.zeros((ctx_len, LANE), F32), sin_t], axis=0)
    return cos_t, sin_t


def kernel(x, c, ctx, c_ctx, w_ada, b_ada, norm_g, w_in, w_out, rw_w0, rw_w_up, rw_a0, rw_a_up,
           rw_k_k, rw_k_a, rw_r_k, rw_ln_g, rw_ln_b, at_sink, cv_w, final_g):
    b, seq, _ = x.shape
    ctx_len = ctx.shape[1]
    depth = w_in.shape[0]
    assert ctx_len % TM == 0 and seq % TM == 0 and b + 1 <= 8
    nct = ctx_len // TM

    xa = (x, ctx)
    cvec = jnp.zeros((8, D_MODEL), F32).at[0:b].set(c).at[b].set(c_ctx)
    mod = _ada(cvec, w_ada, b_ada)
    cos_t, sin_t = _rope_tables(seq, ctx_len)
    band = _attn_band(ctx_len)

    for l in range(depth):
        mod_l = mod[l].reshape(8, 1, 3 * D_MODEL)
        p_rw, p_q, p_kv, p_cv = _proj(xa, mod_l, norm_g[l].reshape(1, -1), w_in, l, cos_t, sin_t, ctx_len)
        y_f, y_b = _rwkv(p_rw, rw_w0[l], rw_w_up[l].reshape(2 * HEAD_DIM, RW_DIM), rw_a0[l],
                         rw_a_up[l].reshape(2 * HEAD_DIM, RW_DIM), rw_k_k[l].reshape(1, -1),
                         rw_k_a[l].reshape(1, -1), nct)
        xa = _out(xa, p_q, p_kv, at_sink[l], band, mod_l, p_rw, y_f, y_b, p_cv, rw_r_k[l].reshape(1, -1),
                  rw_ln_g[l].reshape(1, -1), rw_ln_b[l].reshape(1, -1), cv_w[l], w_out, l,
                  final_g.reshape(1, -1), nct, ctx_len, final=(l == depth - 1))
    return xa
```

```python
import functools

import jax
import jax.numpy as jnp
from jax import lax
from jax.experimental import pallas as pl
from jax.experimental.pallas import tpu as pltpu

F32 = jnp.float32
BF16 = jnp.bfloat16

D_MODEL = 1024
HEAD_DIM = 64
GRID_W = 64
ROPE_THETA = 10000.0
RW_DIM = 384
AT_DIM = 384
CV_DIM = 256
WINDOW = 128
NORM_EPS = 1e-6
LNX_EPS = 64e-5
MASK_VALUE = -1e30
DECAY_SCALE = 0.6065306597126334
LOG2E = 1.4426950408889634
Q_SCALE = HEAD_DIM ** -0.5 * LOG2E

COL_RW = (0, 1792)
COL_Q = (1792, 2176)
COL_KV = (2176, 2432)
COL_CV = (2432, 3840)
D_IN = 3840

TM = 256
PROJ_TILES = (768, 512, 256)
CHUNK = 64
RW_TILES = 3
QB = 128
LANE = 128
HALO = 16
GROUP_W = 256
VMEM_LIMIT = 56 * 1024 * 1024


def _mm(a, b):
    return jnp.dot(a, b, preferred_element_type=F32)


def _mm_nt(a, b):
    return lax.dot_general(a, b, (((1,), (1,)), ((), ())), preferred_element_type=F32)


def _dot_split(a, b, passes=2, left_exact=False):
    acc = None
    rem = a if not left_exact else b
    for p in range(passes):
        hi = rem.astype(BF16)
        t = jnp.dot(a, hi, preferred_element_type=F32) if left_exact else jnp.dot(hi, b, preferred_element_type=F32)
        acc = t if acc is None else acc + t
        if p + 1 < passes:
            rem = rem - hi.astype(F32)
    return acc


def _sigmoid(x):
    return 1.0 / (1.0 + jnp.exp(-x))


def _head_ones(n, scale=1.0):
    r = lax.broadcasted_iota(jnp.int32, (n, n), 0) // HEAD_DIM
    c = lax.broadcasted_iota(jnp.int32, (n, n), 1) // HEAD_DIM
    return jnp.where(r == c, scale, 0.0).astype(BF16)


def _head_sum(x, scale=1.0, passes=2):
    parts = []
    for lo in range(0, x.shape[1], GROUP_W):
        hi = min(lo + GROUP_W, x.shape[1])
        parts.append(_dot_split(x[:, lo:hi], _head_ones(hi - lo, scale), passes))
    return jnp.concatenate(parts, axis=1)


def _ada_kernel(c_ref, w_ref, b_ref, o_ref):
    c = c_ref[...]
    a = c * _sigmoid(c)
    w = w_ref[...]
    a_hi = a.astype(BF16)
    w_hi = w.astype(BF16)
    a_lo = (a - a_hi.astype(F32)).astype(BF16)
    w_lo = (w - w_hi.astype(F32)).astype(BF16)
    o_ref[...] = _mm(a_hi, w_hi) + _mm(a_lo, w_hi) + _mm(a_hi, w_lo) + b_ref[...]


def _ada(cvec, w_ada, b_ada):
    depth = w_ada.shape[0]
    tn = 1024
    return pl.pallas_call(
        _ada_kernel,
        grid=(depth, 3 * D_MODEL // tn),
        in_specs=[
            pl.BlockSpec((8, D_MODEL), lambda l, n: (0, 0)),
            pl.BlockSpec((None, D_MODEL, tn), lambda l, n: (l, 0, n)),
            pl.BlockSpec((None, 1, tn), lambda l, n: (l, 0, n)),
        ],
        out_specs=pl.BlockSpec((None, 8, tn), lambda l, n: (l, 0, n)),
        out_shape=jax.ShapeDtypeStruct((depth, 8, 3 * D_MODEL), F32),
        compiler_params=pltpu.CompilerParams(vmem_limit_bytes=VMEM_LIMIT),
        name="ada",
    )(cvec, w_ada, b_ada.reshape(depth, 1, 3 * D_MODEL))


def _stream_tile(x_refs, tile0, nct, nsub):
    if len(x_refs) == 1:
        return x_refs[0][...]
    blocks = []
    for k in range(nsub):
        blk = x_refs[k][...]
        if nsub + k < len(x_refs):
            blk = jnp.where(tile0 + k < nct, x_refs[nsub + k][...], blk)
        blocks.append(blk)
    return blocks[0] if nsub == 1 else jnp.concatenate(blocks, axis=0)


def _stream_specs(split, nsub, nct, off=0):
    if not split:
        return [pl.BlockSpec((None, nsub * TM, D_MODEL), lambda bi, i: (bi, i + off, 0))]
    lat = [pl.BlockSpec((None, TM, D_MODEL),
                        lambda bi, i, k=k: (bi, jnp.maximum((i + off) * nsub + k - nct, 0), 0)) for k in range(nsub)]
    ctx = [pl.BlockSpec((None, TM, D_MODEL),
                        lambda bi, i, k=k: (bi, jnp.minimum((i + off) * nsub + k, nct - 1), 0))
           for k in range(min(nsub, nct))]
    return lat + ctx


def _proj_kernel(*refs, ctx_len, nx):
    x_refs, (modb_ref, modc_ref, g_ref, w_ref, cos_ref, sin_ref, rw_ref, q_ref, kv_ref, cv_ref) = refs[:nx], refs[nx:]
    tm = rw_ref.shape[0]
    i = pl.program_id(1)
    x = _stream_tile(x_refs, i * (tm // TM), ctx_len // TM, tm // TM)
    _proj_compute(x, i * tm, modb_ref, modc_ref, g_ref, w_ref, cos_ref, sin_ref,
                  rw_ref, q_ref, kv_ref, cv_ref, ctx_len)


def _proj_compute(x, row0, modb_ref, modc_ref, g_ref, w_ref, cos_ref, sin_ref, rw_ref, q_ref, kv_ref, cv_ref,
                  ctx_len):
    tm = x.shape[0]
    ms = jnp.mean(x * x, axis=-1, keepdims=True)
    y = x * lax.rsqrt(ms + NORM_EPS) * g_ref[...]
    mb = modb_ref[...]
    mc = modc_ref[...]
    blocks = []
    for k in range(tm // TM):
        is_ctx = row0 + k * TM < ctx_len
        shift = jnp.where(is_ctx, mc[:, 0:D_MODEL], mb[:, 0:D_MODEL])
        gain = 1.0 + jnp.where(is_ctx, mc[:, D_MODEL:2 * D_MODEL], mb[:, D_MODEL:2 * D_MODEL])
        blocks.append(y[k * TM:(k + 1) * TM] * gain + shift)
    h = blocks[0] if len(blocks) == 1 else jnp.concatenate(blocks, axis=0)

    def seg(cols):
        return jnp.dot(h, w_ref[:, cols[0]:cols[1]], preferred_element_type=F32)

    cos = cos_ref[...]
    sin = sin_ref[...]
    lane = lax.broadcasted_iota(jnp.int32, (tm, LANE), 1)
    first_half = (lane % HEAD_DIM) < (HEAD_DIM // 2)
    low_head = lane < HEAD_DIM

    def rope(t):
        rot = jnp.where(first_half, pltpu.roll(t, LANE - 32, 1), pltpu.roll(t, 32, 1))
        return t * cos + rot * sin

    def dup(t):
        sw = pltpu.roll(t, HEAD_DIM, 1)
        return jnp.where(low_head, t, sw), jnp.where(low_head, sw, t)

    q = seg(COL_Q)
    q_ref[...] = jnp.concatenate(
        [rope(q[:, LANE * j:LANE * (j + 1)]) * Q_SCALE for j in range(AT_DIM // LANE)],
        axis=1).astype(BF16)
    kv = seg(COL_KV)
    k0, k1 = dup(rope(kv[:, 0:LANE]))
    v0, v1 = dup(kv[:, LANE:2 * LANE])
    kv_ref[...] = jnp.concatenate([k0, k1, v0, v1], axis=1).astype(BF16)
    rw_ref[...] = seg(COL_RW).astype(BF16)
    cv_ref[...] = seg(COL_CV).astype(BF16)


def _stream_operands(stream, nsub):
    if isinstance(stream, tuple):
        x, ctx = stream
        return [x] * nsub + [ctx] * min(nsub, ctx.shape[1] // TM), True, x.shape[0], x.shape[1] + ctx.shape[1]
    return [stream], False, stream.shape[0], stream.shape[1]


def _proj(stream, mod, norm_g, w_in, layer, cos_t, sin_t, ctx_len):
    _, split, b, r = _stream_operands(stream, 1)
    tm = next(t for t in PROJ_TILES if r % t == 0)
    nsub = tm // TM
    x_ops = _stream_operands(stream, nsub)[0]
    x_specs = _stream_specs(split, nsub, ctx_len // TM)
    row = lambda bi, i: (bi, i, 0)
    widths = (COL_RW[1] - COL_RW[0], AT_DIM, 4 * LANE, COL_CV[1] - COL_CV[0])
    return pl.pallas_call(
        functools.partial(_proj_kernel, ctx_len=ctx_len, nx=len(x_specs)),
        grid=(b, r // tm),
        in_specs=x_specs + [
            pl.BlockSpec((None, 1, 3 * D_MODEL), lambda bi, i: (bi, 0, 0)),
            pl.BlockSpec((None, 1, 3 * D_MODEL), lambda bi, i: (b, 0, 0)),
            pl.BlockSpec((1, D_MODEL), lambda bi, i: (0, 0)),
            pl.BlockSpec((None, D_MODEL, D_IN), lambda bi, i: (layer, 0, 0), pipeline_mode=pl.Buffered(1)),
            pl.BlockSpec((tm, LANE), lambda bi, i: (i, 0)),
            pl.BlockSpec((tm, LANE), lambda bi, i: (i, 0)),
        ],
        out_specs=[pl.BlockSpec((None, tm, w), row) for w in widths],
        out_shape=[jax.ShapeDtypeStruct((b, r, w), BF16) for w in widths],
        compiler_params=pltpu.CompilerParams(
            dimension_semantics=("parallel", "parallel"), vmem_limit_bytes=VMEM_LIMIT),
        name="proj",
    )(*x_ops, mod, mod, norm_g, w_in, cos_t, sin_t)


_RW_FIELDS = ("at", "rt", "kt", "bt", "kh", "bh", "v")


def _group_masks(rev_lo, rev_hi):
    w = GROUP_W
    row = lax.broadcasted_iota(jnp.int32, (CHUNK, w), 0)
    lane = lax.broadcasted_iota(jnp.int32, (CHUNK, w), 1)
    s = lane % CHUNK
    x = row ^ s
    if rev_lo == rev_hi:
        before = (row - s) if rev_lo else (s - row)
    else:
        sign = jnp.where(lane >= LANE, -1 if rev_hi else 1, -1 if rev_lo else 1)
        before = (s - row) * sign
    m = {
        "strict": before < 0,
        "incl": before <= 0,
        "eye": s == row,
        "lvl1": (x >> 1) == 0,
        "heads": [(lane // HEAD_DIM) == h for h in range(w // HEAD_DIM)],
    }
    for lg in range(1, 6):
        m["lvl%d" % (1 << lg)] = (x >> lg) == 1
    return m


def _bd(x, m):
    return jnp.concatenate([jnp.where(hm, x, 0.0) for hm in m["heads"]], axis=0)


def _head_transpose(x):
    xt = x.T
    return jnp.concatenate([xt[h * HEAD_DIM:(h + 1) * HEAD_DIM] for h in range(x.shape[1] // HEAD_DIM)], axis=1)


def _chunk_pieces(ch, side=()):
    def advance():
        for gen in side:
            next(gen, None)

    n = range(len(ch))
    ms = [c["m"] for c in ch]
    advance()
    ar = [jnp.concatenate([c["at"], c["rt"]], axis=0) for c in ch]
    gb = [_mm(ar[i], _bd(_head_transpose(ch[i]["bt"]), ms[i])) for i in n]
    gk = [_mm(ar[i], _bd(_head_transpose(ch[i]["kt"]), ms[i])) for i in n]
    lo = [jnp.where(ms[i]["strict"], gb[i][:CHUNK], 0.0) for i in n]
    aak = [jnp.where(ms[i]["strict"], gk[i][:CHUNK], 0.0) for i in n]
    arb = [jnp.where(ms[i]["incl"], gb[i][CHUNK:], 0.0) for i in n]
    ark = [jnp.where(ms[i]["incl"], gk[i][CHUNK:], 0.0) for i in n]

    t = [jnp.where(ms[i]["eye"], 1.0, 0.0) - jnp.where(ms[i]["lvl1"], lo[i], 0.0) for i in n]
    for lvl in (2, 4, 8, 16, 32):
        x = [_mm(t[i], _bd(jnp.where(ms[i]["lvl%d" % lvl], lo[i], 0.0), ms[i])) for i in n]
        t = [t[i] - _mm(x[i], _bd(t[i], ms[i])) for i in n]
        advance()

    c2 = 2 * CHUNK
    bht = [_head_transpose(c["bh"]) for c in ch]
    kht = [_head_transpose(c["kh"]) for c in ch]
    av = [_mm(jnp.concatenate([aak[i], ark[i], kht[i]], axis=0), _bd(ch[i]["v"], ms[i])) for i in n]
    z = [_mm(jnp.concatenate([arb[i], bht[i]], axis=0), _bd(t[i], ms[i])) for i in n]
    pw = [_mm(z[i], _bd(ch[i]["at"], ms[i])) for i in n]
    pu = [_mm(z[i], _bd(av[i][:CHUNK], ms[i])) for i in n]
    qh = [ch[i]["rt"] - pw[i][:CHUNK] for i in n]
    yi = [av[i][CHUNK:c2] - pu[i][:CHUNK] for i in n]
    mm = [jnp.where(ms[i]["eye"], ch[i]["pe"], 0.0) - pw[i][CHUNK:] for i in n]
    nn = [av[i][c2:] - pu[i][CHUNK:] for i in n]
    return qh, yi, mm, nn


_RW_SLOT_FIELDS = _RW_FIELDS + ("pe",)
_RW_LAYOUT = (((0, 0, 256),), ((0, 256, 384), (1, 0, 128)), ((1, 128, 384),))


def _rwkv_prep(x_ref, d, w0_ref, wup_ref, a0_ref, aup_ref, kk_ref, ka_ref, dst):
    nchunk = TM // CHUNK
    rev = d == 1
    rt_i = lax.broadcasted_iota(jnp.int32, (TM, TM), 0)
    ct_i = lax.broadcasted_iota(jnp.int32, (TM, TM), 1)
    same_chunk = (rt_i // CHUNK) == (ct_i // CHUNK)
    lane = lax.broadcasted_iota(jnp.int32, (TM, LANE), 1)
    r = x_ref[:, 0:384].astype(F32)
    k = x_ref[:, 384:768].astype(F32)
    dmask = (lane // HEAD_DIM) == d
    tw = jnp.where(dmask, jnp.tanh(x_ref[:, 1536:1664].astype(F32)), 0.0)
    ta = jnp.where(dmask, x_ref[:, 1664:1792].astype(F32), 0.0)
    lw = w0_ref[d:d + 1, :] + _mm(tw, wup_ref[...])
    la = a0_ref[d:d + 1, :] + _mm(ta, aup_ref[...])
    kk = k * kk_ref[...]
    ss = _head_sum(kk * kk, passes=1)
    yield
    logw = -(DECAY_SCALE * LOG2E) * _sigmoid(lw)
    tri = jnp.where(same_chunk & ((ct_i >= rt_i) if rev else (ct_i <= rt_i)), 1.0, 0.0).astype(BF16)
    cl = _dot_split(tri, logw, left_exact=True)
    yield
    a = _sigmoid(la)
    kk = kk * lax.rsqrt(jnp.maximum(ss, 1e-12))
    kd = k * (1.0 + (a - 1.0) * ka_ref[...])
    beta = kk * a
    ends = [(c * CHUNK) if rev else (c * CHUNK + CHUNK - 1) for c in range(nchunk)]
    pe = jnp.concatenate([jnp.broadcast_to(jnp.exp2(cl[e:e + 1, :]), (CHUNK, RW_DIM)) for e in ends], axis=0)
    inv = jnp.exp2(-cl)
    kt = kd * inv
    bt = beta * inv
    dst["at"][...] = kk * jnp.exp2(cl - logw)
    dst["rt"][...] = r * jnp.exp2(cl)
    dst["kt"][...] = kt
    dst["bt"][...] = bt
    dst["kh"][...] = kt * pe
    dst["bh"][...] = bt * pe
    dst["v"][...] = x_ref[:, 768:1152].astype(F32)
    dst["pe"][...] = pe


def _rwkv_chains(src, masks, groups, row_off):
    nchunk = TM // CHUNK
    chains = []
    for step in range(nchunk):
        rows = (slice(step * CHUNK, (step + 1) * CHUNK),
                slice((nchunk - 1 - step) * CHUNK, (nchunk - step) * CHUNK))
        for g in groups:
            parts = _RW_LAYOUT[g]

            def gather(name):
                cols = [src[d][name][rows[d], lo:hi] for d, lo, hi in parts]
                return cols[0] if len(cols) == 1 else jnp.concatenate(cols, axis=1)
            ch = {name: gather(name) for name in _RW_FIELDS}
            ch.update(pe=gather("pe")[0:1, :], m=masks[g], g=g, step=step,
                      dst=[(d, slice(rows[d].start + row_off, rows[d].stop + row_off), lo, hi)
                           for d, lo, hi in parts])
            chains.append(ch)
    return chains


def _rwkv_state_pass(chains, pieces, yf_ref, yb_ref, h_refs):
    qh, yi, mm, nn = pieces
    y_refs = (yf_ref, yb_ref)
    h = [h_ref[...] for h_ref in h_refs]
    for step in range(TM // CHUNK):
        idx = [i for i, c in enumerate(chains) if c["step"] == step]
        res = {i: _mm(jnp.concatenate([qh[i], mm[i]], axis=0), _bd(h[chains[i]["g"]], chains[i]["m"]))
               for i in idx}
        for i in idx:
            ys = res[i][:CHUNK] + yi[i]
            off = 0
            for d, rws, lo, hi in chains[i]["dst"]:
                y_refs[d][rws, lo:hi] = ys[:, off:off + hi - lo].astype(yf_ref.dtype)
                off += hi - lo
            h[chains[i]["g"]] = res[i][CHUNK:] + nn[i]
        yield
    for h_ref, hv in zip(h_refs, h):
        h_ref[...] = hv


def _drain(*gens):
    for gen in gens:
        for _ in gen:
            pass


def _rwkv_kernel(*refs):
    x_refs, refs = refs[:2 * RW_TILES], refs[2 * RW_TILES:]
    params, (yf_ref, yb_ref), scratch = refs[:6], refs[6:8], refs[8:]
    ngroup = len(_RW_LAYOUT)
    h_refs = scratch[:ngroup]
    nf = len(_RW_SLOT_FIELDS)
    prepared = [[dict(zip(_RW_SLOT_FIELDS, scratch[ngroup + (t * 2 + d) * nf:ngroup + (t * 2 + d + 1) * nf]))
                 for d in range(2)] for t in range(RW_TILES)]

    @pl.when(pl.program_id(1) == 0)
    def _():
        for ref in h_refs:
            ref[...] = jnp.zeros_like(ref)

    masks = [_group_masks(False, False), _group_masks(False, True), _group_masks(True, True)]

    def prep(t, d):
        return _rwkv_prep(x_refs[d * RW_TILES + t], d, *params, prepared[t][d])

    _drain(prep(0, 0))
    pending = ()
    for t in range(RW_TILES):
        side = (prep(t, 1),) + pending
        chains_a = _rwkv_chains(prepared[t], masks, (0,), t * TM)
        pieces_a = _chunk_pieces(chains_a, side)
        _drain(*side)
        side = (prep(t + 1, 0),) if t + 1 < RW_TILES else ()
        chains_b = _rwkv_chains(prepared[t], masks, (1, 2), t * TM)
        pieces_b = _chunk_pieces(chains_b, side)
        _drain(*side)
        pieces = [a + b for a, b in zip(pieces_a, pieces_b)]
        pending = (_rwkv_state_pass(chains_a + chains_b, pieces, yf_ref, yb_ref, h_refs),)
    _drain(*pending)


def _bwd_tile(i, nct, nt):
    return jnp.where(i < nct, nct - 1 - i, nt - 1 - (i - nct))


def _rwkv(p_rw, w0, wup, a0, aup, k_k, k_a, nct):
    b, r, wcols = p_rw.shape
    nt = r // TM
    steps = -(-nt // RW_TILES)
    src = []
    for d in range(2):
        for t in range(RW_TILES):
            def index(bi, j, t=t, d=d):
                tile = jnp.minimum(j * RW_TILES + t, nt - 1)
                return (bi, _bwd_tile(tile, nct, nt) if d else tile, 0)
            src.append(pl.BlockSpec((None, TM, wcols), index))
    full = lambda shape: pl.BlockSpec(shape, lambda bi, j: (0,) * len(shape))
    dst = pl.BlockSpec((None, RW_TILES * TM, RW_DIM), lambda bi, j: (bi, j, 0))
    scratch = [pltpu.VMEM((HEAD_DIM, GROUP_W), F32) for _ in _RW_LAYOUT]
    scratch += [pltpu.VMEM((TM, RW_DIM), F32) for _ in range(RW_TILES * 2 * len(_RW_SLOT_FIELDS))]
    return pl.pallas_call(
        _rwkv_kernel,
        grid=(b, steps),
        in_specs=src + [
            full((2, RW_DIM)), full((2 * HEAD_DIM, RW_DIM)),
            full((2, RW_DIM)), full((2 * HEAD_DIM, RW_DIM)),
            full((1, RW_DIM)), full((1, RW_DIM)),
        ],
        out_specs=[dst, dst],
        out_shape=[jax.ShapeDtypeStruct((b, steps * RW_TILES * TM, RW_DIM), BF16)] * 2,
        scratch_shapes=scratch,
        compiler_params=pltpu.CompilerParams(
            dimension_semantics=("parallel", "arbitrary"), vmem_limit_bytes=VMEM_LIMIT),
        name="rwkv",
    )(*([p_rw] * (2 * RW_TILES)), w0, wup, a0, aup, k_k, k_a)


def _attn_compute(i, sink_ref, band_ref, q_ref, kp_ref, kc_ref, kn_ref, kx_ref, nct, nt):
    latent = i >= nct
    nloc = 3 * QB
    nkeys = nloc + kx_ref.shape[0]

    lane = lax.broadcasted_iota(jnp.int32, (QB, LANE), 1)
    low = lane < HEAD_DIM
    low_m = jnp.where(low, 1.0, 0.0).astype(BF16)
    high_m = jnp.where(low, 0.0, 1.0).astype(BF16)
    key = lax.broadcasted_iota(jnp.int32, (nkeys, 1), 0)
    hrow = lax.broadcasted_iota(jnp.int32, (3 * QB, 1), 0) // QB
    band = band_ref[...]
    t_row = lax.broadcasted_iota(jnp.int32, (3 * QB, LANE), 0) % QB
    t_hot = jnp.where(t_row == lax.broadcasted_iota(jnp.int32, (3 * QB, LANE), 1), 1.0, 0.0).astype(BF16)
    head_src = (((0, low_m), (0, high_m), (1, low_m)), ((1, high_m), (2, low_m), (2, high_m)))

    qs, kall, vall, sk = [], [], [], []
    for qb in range(TM // QB):
        qrows = slice(qb * QB, (qb + 1) * QB)
        qt = [q_ref[qrows, LANE * t:LANE * (t + 1)] for t in range(AT_DIM // LANE)]
        halves = [(kc_ref, slice(0, QB)), (kc_ref, slice(QB, 2 * QB))]
        if qb == 0:
            blocks = [(kp_ref, slice(0, QB))] + halves
            first_ok, last_ok = i > nct, True
        else:
            blocks = halves + [(kn_ref, slice(0, QB))]
            first_ok, last_ok = True, i < nt - 1
        lo_c = jnp.where(jnp.logical_and(latent, first_ok), 0, jnp.where(latent, QB, nloc))
        hi_c = jnp.where(last_ok, nloc, 2 * QB)
        dead = jnp.logical_and(key < nloc, jnp.logical_or(key < lo_c, key >= hi_c))
        bias_qb = (band + jnp.where(dead, MASK_VALUE, 0.0)).astype(BF16)
        for g in range(2):
            qs.append(jnp.concatenate(
                [jnp.concatenate([qt[t] * hm for t, hm in head_src[g]], axis=0), t_hot], axis=1))
            kcols = slice(LANE * g, LANE * (g + 1))
            vcols = slice(2 * LANE + LANE * g, 2 * LANE + LANE * (g + 1))
            kall.append(jnp.concatenate(
                [jnp.concatenate([ref[rs, kcols] for ref, rs in blocks] + [kx_ref[:, kcols]], axis=0), bias_qb],
                axis=1))
            vall.append(jnp.concatenate([ref[rs, vcols] for ref, rs in blocks] + [kx_ref[:, vcols]], axis=0))
            sk.append(LOG2E * jnp.where(hrow == 0, sink_ref[3 * g],
                                        jnp.where(hrow == 1, sink_ref[3 * g + 1], sink_ref[3 * g + 2])))

    n = range(len(qs))
    s = [_mm_nt(qs[c], kall[c]) for c in n]
    ps, dens = [[] for _ in n], [[] for _ in n]
    for h in range(3):
        rows = slice(h * QB, (h + 1) * QB)
        mx = [jnp.maximum(jnp.max(s[c][rows], axis=-1, keepdims=True), sk[c][rows]) for c in n]
        ph = [jnp.exp2(s[c][rows] - mx[c]) for c in n]
        for c in n:
            dens[c].append(jnp.sum(ph[c], axis=-1, keepdims=True) + jnp.exp2(sk[c][rows] - mx[c]))
            ps[c].append(ph[c].astype(BF16))
    o = [_mm(jnp.concatenate(ps[c], axis=0), vall[c]) / jnp.concatenate(dens[c], axis=0) for c in n]

    blocks = []
    for qb in range(TM // QB):
        o0, o1 = o[2 * qb], o[2 * qb + 1]
        blocks.append(jnp.concatenate([
            jnp.where(low, o0[0:QB], o0[QB:2 * QB]),
            jnp.where(low, o0[2 * QB:3 * QB], o1[0:QB]),
            jnp.where(low, o1[QB:2 * QB], o1[2 * QB:3 * QB]),
        ], axis=1))
    return jnp.concatenate(blocks, axis=0)


def _attn_band(ctx_len):
    key = jnp.arange(3 * QB + ctx_len)[:, None]
    t = jnp.arange(QB)[None, :]
    ok = ((key >= t) & (key <= t + 2 * WINDOW)) | (key >= 3 * QB)
    return jnp.where(ok, 0.0, MASK_VALUE).astype(F32)


def _attn_specs(r, off, ctx_len, band_shape, kvw):
    nct = ctx_len // TM
    per = TM // QB
    nb = r // QB
    clamp = lambda j: jnp.clip(j, nct * per, nb - 1)
    return [
        pl.BlockSpec(memory_space=pltpu.SMEM),
        pl.BlockSpec(band_shape, lambda bi, i: (0, 0)),
        pl.BlockSpec((None, TM, AT_DIM), lambda bi, i: (bi, i + off, 0)),
        pl.BlockSpec((None, QB, kvw), lambda bi, i: (bi, clamp((i + off) * per - 1), 0)),
        pl.BlockSpec((None, TM, kvw), lambda bi, i: (bi, i + off, 0)),
        pl.BlockSpec((None, QB, kvw), lambda bi, i: (bi, clamp((i + off + 1) * per), 0)),
        pl.BlockSpec((None, ctx_len, kvw), lambda bi, i: (bi, 0, 0)),
    ]


def _out_kernel(*refs, nx, tile_off, ctx_len, rows, final):
    x_refs, attn_refs, rest = refs[:nx], refs[nx:nx + 7], refs[nx + 7:]
    fg_ref, o_ref = rest[-2:]
    tile = pl.program_id(1) + tile_off
    x = _stream_tile(x_refs, tile, ctx_len // TM, 1)
    partial, at_gate = _mix_compute(tile * TM, *rest[1:-2], ctx_len, rows)
    y_at = _attn_compute(tile, *attn_refs, ctx_len // TM, rows // TM)
    mod_ref, wout_ref = rest[0], rest[-3]
    acc = partial + _mm(y_at * at_gate, wout_ref[384:768, :])
    xn = x + mod_ref[:, 2 * D_MODEL:3 * D_MODEL] * acc
    if final:
        ms = jnp.mean(xn * xn, axis=-1, keepdims=True)
        xn = xn * lax.rsqrt(ms + NORM_EPS) * fg_ref[...]
    o_ref[...] = xn


def _mix_compute(row0, rw_ref, yf_ref, yb_ref, cv_ref, cvp_ref, cvn_ref,
                 rk_ref, lng_ref, lnb_ref, cvw_ref, wout_ref, ctx_len, rows):
    r = rw_ref[:, 0:384].astype(F32)
    k = rw_ref[:, 384:768].astype(F32)
    v = rw_ref[:, 768:1152].astype(F32)
    g = rw_ref[:, 1152:1536].astype(F32)

    y = yf_ref[...].astype(F32) + yb_ref[...].astype(F32)
    dev = y - _head_sum(y, 1.0 / HEAD_DIM)
    var = _head_sum(dev * dev, 1.0 / HEAD_DIM, passes=1)
    yn = dev * lax.rsqrt(var + LNX_EPS) * lng_ref[...] + lnb_ref[...]
    bonus = _head_sum(r * k * rk_ref[...], passes=1) * v
    y_rw = (yn + bonus) * (g * _sigmoid(g))

    g_at = cv_ref[:, 0:384].astype(F32)
    at_gate = g_at * _sigmoid(g_at)

    bg = cv_ref[:, 384:640].astype(F32)
    u = cv_ref[:, 640:896].astype(F32) * cv_ref[:, 896:1152].astype(F32)
    g_cv = cv_ref[:, 1152:1408].astype(F32)
    prev_ok = jnp.logical_and(row0 != 0, row0 != ctx_len)
    next_ok = jnp.logical_and(row0 + TM != ctx_len, row0 + TM != rows)
    pu = cvp_ref[:, 640:896].astype(F32) * cvp_ref[:, 896:1152].astype(F32)
    nu = cvn_ref[:, 640:896].astype(F32) * cvn_ref[:, 896:1152].astype(F32)
    u_prev = jnp.where(prev_ok, pu[HALO - 1:HALO, :], 0.0)
    u_next = jnp.where(next_ok, nu[0:1, :], 0.0)
    ridx = lax.broadcasted_iota(jnp.int32, (TM, CV_DIM), 0)
    um = jnp.where(ridx == 0, u_prev, pltpu.roll(u, 1, 0))
    up = jnp.where(ridx == TM - 1, u_next, pltpu.roll(u, TM - 1, 0))
    conv = um * cvw_ref[0:1, :] + u * cvw_ref[1:2, :] + up * cvw_ref[2:3, :]
    y_cv = bg * conv * (g_cv * _sigmoid(g_cv))

    return _mm(y_rw, wout_ref[0:384, :]) + _mm(y_cv, wout_ref[768:1024, :]), at_gate


def _out_in_specs(b, r, off, nct, cvw, layer):
    hb = TM // HALO
    row = lambda bi, i: (bi, i + off, 0)
    full = lambda shape: pl.BlockSpec(shape, lambda bi, i: (0,) * len(shape))
    return [
        pl.BlockSpec((None, 1, 3 * D_MODEL), lambda bi, i: (jnp.where(i + off < nct, b, bi), 0, 0)),
        pl.BlockSpec((None, TM, 1536), row),
        pl.BlockSpec((None, TM, RW_DIM), row),
        pl.BlockSpec((None, TM, RW_DIM), lambda bi, i: (bi, _bwd_tile(i + off, nct, r // TM), 0)),
        pl.BlockSpec((None, TM, cvw), row),
        pl.BlockSpec((None, HALO, cvw), lambda bi, i: (bi, jnp.maximum((i + off) * hb - 1, 0), 0)),
        pl.BlockSpec((None, HALO, cvw), lambda bi, i: (bi, jnp.minimum((i + off + 1) * hb, r // HALO - 1), 0)),
        full((1, RW_DIM)), full((1, RW_DIM)), full((1, RW_DIM)),
        full((3, CV_DIM)),
        pl.BlockSpec((None, D_MODEL, D_MODEL), lambda bi, i: (layer, 0, 0)),
    ]


def _out(stream, p_q, p_kv, sink, band, mod, p_rw, y_f, y_b, p_cv, r_k, ln_g, ln_b, cv_w, w_out, layer, final_g,
         nct, ctx_len, final):
    x_ops, split, b, r = _stream_operands(stream, 1)
    nt = r // TM
    off = nct if final else 0
    x_specs = _stream_specs(split, 1, nct, off)
    return pl.pallas_call(
        functools.partial(_out_kernel, nx=len(x_specs), tile_off=off, ctx_len=ctx_len, rows=r, final=final),
        grid=(b, nt - off),
        in_specs=x_specs + _attn_specs(r, off, ctx_len, band.shape, p_kv.shape[-1])
        + _out_in_specs(b, r, off, nct, p_cv.shape[-1], layer)
        + [pl.BlockSpec((1, D_MODEL), lambda bi, i: (0, 0))],
        out_specs=pl.BlockSpec((None, TM, D_MODEL), lambda bi, i: (bi, i, 0)),
        out_shape=jax.ShapeDtypeStruct((b, r - off * TM, D_MODEL), F32),
        compiler_params=pltpu.CompilerParams(
            dimension_semantics=("parallel", "parallel"), vmem_limit_bytes=VMEM_LIMIT),
        name="out",
    )(*x_ops, sink, band, p_q, p_kv, p_kv, p_kv, p_kv,
      mod, p_rw, y_f, y_b, p_cv, p_cv, p_cv, r_k, ln_g, ln_b, cv_w, w_out, final_g)


def _rope_tables(seq, ctx_len):
    t = jnp.arange(seq)
    nfreq = HEAD_DIM // 4
    inv = ROPE_THETA ** (-jnp.arange(nfreq, dtype=F32) / nfreq)
    ang = jnp.concatenate([(t // GRID_W).astype(F32)[:, None] * inv,
                           (t % GRID_W).astype(F32)[:, None] * inv], axis=-1)
    cos, sin = jnp.cos(ang), jnp.sin(ang)
    reps = LANE // HEAD_DIM
    cos_t = jnp.tile(jnp.concatenate([cos, cos], axis=-1), (1, reps))
    sin_t = jnp.tile(jnp.concatenate([-sin, sin], axis=-1), (1, reps))
    cos_t = jnp.concatenate([jnp.ones((ctx_len, LANE), F32), cos_t], axis=0)
    sin_t = jnp.concatenate([jnp.zeros((ctx_len, LANE), F32), sin_t], axis=0)
    return cos_t, sin_t


def kernel(x, c, ctx, c_ctx, w_ada, b_ada, norm_g, w_in, w_out, rw_w0, rw_w_up, rw_a0, rw_a_up,
           rw_k_k, rw_k_a, rw_r_k, rw_ln_g, rw_ln_b, at_sink, cv_w, final_g):
    b, seq, _ = x.shape
    ctx_len = ctx.shape[1]
    depth = w_in.shape[0]
    assert ctx_len % TM == 0 and seq % TM == 0 and b + 1 <= 8
    nct = ctx_len // TM

    xa = (x, ctx)
    cvec = jnp.zeros((8, D_MODEL), F32).at[0:b].set(c).at[b].set(c_ctx)
    mod = _ada(cvec, w_ada, b_ada)
    cos_t, sin_t = _rope_tables(seq, ctx_len)
    band = _attn_band(ctx_len)

    for l in range(depth):
        mod_l = mod[l].reshape(8, 1, 3 * D_MODEL)
        p_rw, p_q, p_kv, p_cv = _proj(xa, mod_l, norm_g[l].reshape(1, -1), w_in, l, cos_t, sin_t, ctx_len)
        y_f, y_b = _rwkv(p_rw, rw_w0[l], rw_w_up[l].reshape(2 * HEAD_DIM, RW_DIM), rw_a0[l],
                         rw_a_up[l].reshape(2 * HEAD_DIM, RW_DIM), rw_k_k[l].reshape(1, -1),
                         rw_k_a[l].reshape(1, -1), nct)
        xa = _out(xa, p_q, p_kv, at_sink[l], band, mod_l, p_rw, y_f, y_b, p_cv, rw_r_k[l].reshape(1, -1),
                  rw_ln_g[l].reshape(1, -1), rw_ln_b[l].reshape(1, -1), cv_w[l], w_out, l,
                  final_g.reshape(1, -1), nct, ctx_len, final=(l == depth - 1))
    return xa
```
